```python
import jax, jax.numpy as jnp
from jax import lax
import numpy as np

D_MODEL = 2048
BATCH = 1
SEQ = 16384
DEPTH = 2

D_MIX = D_MODEL
MLA_HEADS = 8
QK_NOPE_DIM = 128
QK_ROPE_DIM = 64
V_HEAD_DIM = 128
QK_HEAD_DIM = QK_NOPE_DIM + QK_ROPE_DIM
Q_LORA_RANK = 512
KV_LORA_RANK = 256
MLA_WIDTH = MLA_HEADS * V_HEAD_DIM
FNET_GROUPS = 8
FNET_WIDTH = D_MIX - MLA_WIDTH
FNET_GROUP_DIM = FNET_WIDTH // FNET_GROUPS
D_IN = Q_LORA_RANK + KV_LORA_RANK + QK_ROPE_DIM + FNET_WIDTH
ROPE_THETA = 10000.0
Q_BLOCK = 128
D_FF_DENSE = 5632
N_EXPERTS = 8
TOP_K = 2
D_FF_EXPERT = 7168
EXPERT_BLOCK = 512
N_DENSE = (DEPTH + 1) // 2
N_MOE = DEPTH // 2
N_MOD = 6
RMS_EPS = 1e-6

kernel_name = "hybrid_mla_fnet_adaln_moe_encoder"


def rms_norm(x, g):
    xf = x.astype(jnp.float32)
    y = xf * lax.rsqrt(jnp.mean(xf * xf, axis=-1, keepdims=True) + RMS_EPS)
    return (y * g.astype(jnp.float32)).astype(x.dtype)


def rope_tables(positions):
    half = QK_ROPE_DIM // 2
    inv_freq = ROPE_THETA ** (-jnp.arange(half, dtype=jnp.float32) / half)
    ang = positions.astype(jnp.float32)[..., None] * inv_freq
    return jnp.cos(ang), jnp.sin(ang)


def apply_rope(x, cos, sin):
    x1, x2 = jnp.split(x.astype(jnp.float32), 2, axis=-1)
    return jnp.concatenate([x1 * cos - x2 * sin, x2 * cos + x1 * sin], axis=-1).astype(x.dtype)


def swiglu(h, w_gate, w_up, w_down):
    return (jax.nn.silu(h @ w_gate) * (h @ w_up)) @ w_down


def bidir_attention(q, k, v):
    B, H, S, Dq = q.shape
    nb = S // Q_BLOCK
    qb = q.reshape(B, H, nb, Q_BLOCK, Dq).transpose(2, 0, 1, 3, 4)
    scale = QK_HEAD_DIM ** -0.5

    def one_block(qi):
        s = jnp.einsum('bhqd,bhkd->bhqk', qi, k, preferred_element_type=jnp.float32) * scale
        p = jax.nn.softmax(s, axis=-1)
        return jnp.einsum('bhqk,bhkd->bhqd', p.astype(v.dtype), v)

    o = lax.map(one_block, qb)
    return o.transpose(1, 2, 0, 3, 4).reshape(B, H, S, V_HEAD_DIM)


def mla_mix(c_q, c_kv, k_rope, cos, sin, cq_g, ckv_g, w_uq, w_ukv, q_g, k_g):
    B, S, _ = c_q.shape
    q = (rms_norm(c_q, cq_g) @ w_uq).reshape(B, S, MLA_HEADS, QK_HEAD_DIM)
    kv = (rms_norm(c_kv, ckv_g) @ w_ukv).reshape(B, S, MLA_HEADS, QK_NOPE_DIM + V_HEAD_DIM)
    k_nope, v = jnp.split(kv, [QK_NOPE_DIM], axis=-1)
    k_r = jnp.broadcast_to(k_rope[:, :, None, :], (B, S, MLA_HEADS, QK_ROPE_DIM))
    k = jnp.concatenate([k_nope, k_r], axis=-1)
    q = rms_norm(q, q_g)
    k = rms_norm(k, k_g)
    cos_h, sin_h = cos[:, :, None, :], sin[:, :, None, :]
    q = jnp.concatenate([q[..., :QK_NOPE_DIM], apply_rope(q[..., QK_NOPE_DIM:], cos_h, sin_h)], axis=-1)
    k = jnp.concatenate([k[..., :QK_NOPE_DIM], apply_rope(k[..., QK_NOPE_DIM:], cos_h, sin_h)], axis=-1)
    o = bidir_attention(q.transpose(0, 2, 1, 3), k.transpose(0, 2, 1, 3), v.transpose(0, 2, 1, 3))
    return o.transpose(0, 2, 1, 3).reshape(B, S, MLA_WIDTH)


def fourier_mix(u, w_f):
    B, S, _ = u.shape
    ug = u.reshape(B, S, FNET_GROUPS, FNET_GROUP_DIM).astype(jnp.float32)
    f = jnp.fft.fft2(ug, axes=(1, 3), norm='ortho').real.astype(u.dtype)
    return jnp.einsum('bsgc,gcd->bsgd', f, w_f).reshape(B, S, FNET_WIDTH)


def mixer_sublayer(x, shift, scale, gate, cos, sin, norm_g, w_in, cq_g, ckv_g, w_uq, w_ukv, q_g, k_g, w_f, out_g, w_out):
    h = rms_norm(x, norm_g) * (1.0 + scale) + shift
    z = h @ w_in
    s1 = Q_LORA_RANK
    s2 = s1 + KV_LORA_RANK
    s3 = s2 + QK_ROPE_DIM
    c_q, c_kv, k_rope, u = jnp.split(z, [s1, s2, s3], axis=-1)
    a = mla_mix(c_q, c_kv, k_rope, cos, sin, cq_g, ckv_g, w_uq, w_ukv, q_g, k_g)
    f = fourier_mix(u, w_f)
    o = jnp.concatenate([rms_norm(a, out_g[:MLA_WIDTH]), rms_norm(f, out_g[MLA_WIDTH:])], axis=-1) @ w_out
    return x + gate * o


def moe_swiglu(h, w_router, w_gate, w_up, w_down):
    B, S, D = h.shape
    T = B * S
    A = T * TOP_K
    hf = h.reshape(T, D)
    logits = jnp.dot(hf.astype(jnp.float32), w_router.astype(jnp.float32))
    top_logit, top_idx = lax.top_k(logits, TOP_K)
    top_w = jax.nn.softmax(top_logit, axis=-1)
    flat_e = top_idx.reshape(A)
    flat_tok = jnp.repeat(jnp.arange(T, dtype=jnp.int32), TOP_K)
    flat_w = top_w.reshape(A)
    order = jnp.argsort(flat_e)
    sorted_e = flat_e[order]
    counts = jnp.bincount(flat_e, length=N_EXPERTS).astype(jnp.int32)
    padded = (counts + EXPERT_BLOCK - 1) // EXPERT_BLOCK * EXPERT_BLOCK
    start = jnp.cumsum(counts) - counts
    pend = jnp.cumsum(padded)
    pstart = pend - padded
    dest = pstart[sorted_e] + jnp.arange(A, dtype=jnp.int32) - start[sorted_e]
    n_blocks = -(-A // EXPERT_BLOCK) + N_EXPERTS
    P = n_blocks * EXPERT_BLOCK
    row_tok = jnp.full((P,), T, jnp.int32).at[dest].set(flat_tok[order])
    row_w = jnp.zeros((P,), jnp.float32).at[dest].set(flat_w[order])
    block_start = jnp.arange(n_blocks, dtype=jnp.int32) * EXPERT_BLOCK
    block_e = jnp.minimum(jnp.searchsorted(pend, block_start, side='right'), N_EXPERTS - 1).astype(jnp.int32)
    h_pad = jnp.concatenate([hf, jnp.zeros((1, D), hf.dtype)], axis=0)
    xb = h_pad[row_tok].reshape(n_blocks, EXPERT_BLOCK, D)

    def expert_block(args):
        xi, e = args
        return swiglu(xi, w_gate[e], w_up[e], w_down[e])

    yb = lax.map(expert_block, (xb, block_e)).reshape(P, D)
    y = jnp.zeros((T + 1, D), jnp.float32).at[row_tok].add(yb.astype(jnp.float32) * row_w[:, None])
    return y[:T].astype(h.dtype).reshape(B, S, D)


def setup_inputs(seed: int = 0) -> dict:
    key = jax.random.key(seed)
    ks = jax.random.split(key, 32)
    f32 = jnp.float32

    def nrm(k, shape, fan_in, mult=1.0):
        return jax.random.normal(k, shape, f32) * (mult * fan_in ** -0.5)

    def gain(k, shape):
        return 1.0 + 0.05 * jax.random.normal(k, shape, f32)

    x = jax.random.normal(ks[0], (BATCH, SEQ, D_MODEL), f32)
    c = jax.random.normal(ks[1], (BATCH, D_MODEL), f32)
    offs = jax.random.randint(ks[2], (BATCH, 1), 0, 1024, dtype=jnp.int32)
    positions = offs + jnp.arange(SEQ, dtype=jnp.int32)[None, :]
    return {
        'x': x,
        'c': c,
        'positions': positions,
        'ada_w': nrm(ks[3], (DEPTH, D_MODEL, N_MOD * D_MODEL), D_MODEL, 0.5),
        'ada_b': 0.02 * jax.random.normal(ks[4], (DEPTH, N_MOD * D_MODEL), f32),
        'mix_norm': gain(ks[5], (DEPTH, D_MODEL)),
        'w_in': nrm(ks[6], (DEPTH, D_MODEL, D_IN), D_MODEL),
        'cq_norm': gain(ks[7], (DEPTH, Q_LORA_RANK)),
        'ckv_norm': gain(ks[8], (DEPTH, KV_LORA_RANK)),
        'w_uq': nrm(ks[9], (DEPTH, Q_LORA_RANK, MLA_HEADS * QK_HEAD_DIM), Q_LORA_RANK),
        'w_ukv': nrm(ks[10], (DEPTH, KV_LORA_RANK, MLA_HEADS * (QK_NOPE_DIM + V_HEAD_DIM)), KV_LORA_RANK),
        'q_norm': gain(ks[11], (DEPTH, QK_HEAD_DIM)),
        'k_norm': gain(ks[12], (DEPTH, QK_HEAD_DIM)),
        'w_fnet': nrm(ks[13], (DEPTH, FNET_GROUPS, FNET_GROUP_DIM, FNET_GROUP_DIM), FNET_GROUP_DIM),
        'out_norm': gain(ks[14], (DEPTH, D_MIX)),
        'w_out': nrm(ks[15], (DEPTH, D_MIX, D_MODEL), D_MIX),
        'ffn_norm': gain(ks[16], (DEPTH, D_MODEL)),
        'dense_w_gate': nrm(ks[17], (N_DENSE, D_MODEL, D_FF_DENSE), D_MODEL),
        'dense_w_up': nrm(ks[18], (N_DENSE, D_MODEL, D_FF_DENSE), D_MODEL),
        'dense_w_down': nrm(ks[19], (N_DENSE, D_FF_DENSE, D_MODEL), D_FF_DENSE),
        'router_w': nrm(ks[20], (N_MOE, D_MODEL, N_EXPERTS), D_MODEL),
        'moe_w_gate': nrm(ks[21], (N_MOE, N_EXPERTS, D_MODEL, D_FF_EXPERT), D_MODEL),
        'moe_w_up': nrm(ks[22], (N_MOE, N_EXPERTS, D_MODEL, D_FF_EXPERT), D_MODEL),
        'moe_w_down': nrm(ks[23], (N_MOE, N_EXPERTS, D_FF_EXPERT, D_MODEL), D_FF_EXPERT),
    }


def reference(x, c, positions, ada_w, ada_b, mix_norm, w_in, cq_norm, ckv_norm, w_uq, w_ukv, q_norm, k_norm,
              w_fnet, out_norm, w_out, ffn_norm, dense_w_gate, dense_w_up, dense_w_down,
              router_w, moe_w_gate, moe_w_up, moe_w_down):
    cos, sin = rope_tables(positions)
    cond = jax.nn.silu(c)
    for l in range(DEPTH):
        mod = (cond @ ada_w[l] + ada_b[l])[:, None, :]
        sh1, sc1, g1, sh2, sc2, g2 = jnp.split(mod, N_MOD, axis=-1)
        x = mixer_sublayer(x, sh1, sc1, g1, cos, sin, mix_norm[l], w_in[l], cq_norm[l], ckv_norm[l],
                           w_uq[l], w_ukv[l], q_norm[l], k_norm[l], w_fnet[l], out_norm[l], w_out[l])
        h = rms_norm(x, ffn_norm[l]) * (1.0 + sc2) + sh2
        if l % 2 == 0:
            j = l // 2
            f = swiglu(h, dense_w_gate[j], dense_w_up[j], dense_w_down[j])
        else:
            j = l // 2
            f = moe_swiglu(h, router_w[j], moe_w_gate[j], moe_w_up[j], moe_w_down[j])
        x = x + g2 * f
    return x
```

```python
import functools
import math

import numpy as np
import jax
import jax.numpy as jnp
from jax import lax
from jax.experimental import pallas as pl
from jax.experimental.pallas import tpu as pltpu

D_MODEL = 2048
DEPTH = 2
MLA_HEADS = 8
QK_NOPE_DIM = 128
QK_ROPE_DIM = 64
V_HEAD_DIM = 128
QK_HEAD_DIM = QK_NOPE_DIM + QK_ROPE_DIM
Q_LORA_RANK = 512
KV_LORA_RANK = 256
MLA_WIDTH = MLA_HEADS * V_HEAD_DIM
FNET_GROUPS = 8
FNET_GROUP_DIM = 128
FNET_WIDTH = FNET_GROUPS * FNET_GROUP_DIM
ROPE_THETA = 10000.0
N_EXPERTS = 8
N_MOD = 6
RMS_EPS = 1e-6

LANES = 128
QK_PAD = 256
VMEM_LIMIT_BYTES = 56 * 1024 * 1024

ADA_TN = 1024
ROPE_TN = 2048
MIX_TM = 256
ATT_TQ = 1024
ATT_TC = 1024
FNET_K1 = 4
FNET_K2 = 4
FFN_TM = 512
FFN_TF = 512
MOE_BLK = 512
MOE_TF = 512
ROW_TM = 256

BF16 = jnp.bfloat16
F32 = jnp.float32
NT_DIMS = (((1,), (1,)), ((), ()))


def _cparams(sem, vmem=VMEM_LIMIT_BYTES):
    return pltpu.CompilerParams(dimension_semantics=sem, vmem_limit_bytes=vmem)


def _dot(a, b):
    return jnp.dot(a, b, preferred_element_type=F32)


def _rms(x, g):
    ms = jnp.mean(x * x, axis=-1, keepdims=True)
    return (x * lax.rsqrt(ms + RMS_EPS)) * g


def _ada_kernel(c_ref, w_ref, b_ref, o_ref, cond_ref):
    c = c_ref[...]
    cond_ref[...] = c * (1.0 / (1.0 + jnp.exp(-c)))
    d = w_ref.shape[1]
    tn = w_ref.shape[2]
    rows = 64

    def body(i, acc):
        r = pl.multiple_of(i * rows, rows)
        prod = w_ref[0, pl.ds(r, rows), :] * cond_ref[pl.ds(r, rows), :]
        for s in range(rows // 8):
            acc = acc + prod[8 * s:8 * s + 8]
        return acc

    acc = lax.fori_loop(0, d // rows, body, jnp.zeros((8, tn), F32))
    o_ref[0] = jnp.sum(acc, axis=0, keepdims=True) + b_ref[0]


def _ada_mod(c_col, ada_w, ada_b3):
    depth, d, n = ada_w.shape
    tn = min(ADA_TN, n)
    return pl.pallas_call(
        _ada_kernel,
        grid=(depth, n // tn),
        in_specs=[
            pl.BlockSpec((d, 1), lambda l, j: (0, 0)),
            pl.BlockSpec((1, d, tn), lambda l, j: (l, 0, j)),
            pl.BlockSpec((1, 1, tn), lambda l, j: (l, 0, j)),
        ],
        out_specs=pl.BlockSpec((1, 1, tn), lambda l, j: (l, 0, j)),
        out_shape=jax.ShapeDtypeStruct((depth, 1, n), F32),
        scratch_shapes=[pltpu.VMEM((d, 1), F32)],
        compiler_params=_cparams(("arbitrary", "arbitrary")),
        name="ada_mod",
    )(c_col, ada_w, ada_b3)


def _rope_kernel(pos_row_ref, pos_col_ref, invf_col_ref, invf_row_ref, sgn_col_ref, sgn_row_ref,
                 cos_t_ref, sin_t_ref, cos4_ref, sin4_ref):
    ang_t = invf_col_ref[...] * pos_row_ref[...]
    cos_t_ref[...] = jnp.cos(ang_t)
    sin_t_ref[...] = jnp.sin(ang_t) * sgn_col_ref[...]
    ang4 = pos_col_ref[...] * invf_row_ref[...]
    cos4_ref[...] = jnp.cos(ang4)
    sin4_ref[...] = jnp.sin(ang4) * sgn_row_ref[...]


def _rope_tables(pos_f32):
    t = pos_f32.shape[0]
    tn = min(ROPE_TN, t)
    half = QK_ROPE_DIM // 2
    inv_freq = ROPE_THETA ** (-jnp.arange(half, dtype=F32) / half)
    invf_col = jnp.tile(inv_freq, 2).reshape(QK_ROPE_DIM, 1)
    invf_row = jnp.tile(inv_freq, LANES // half).reshape(1, LANES)
    sgn_col = jnp.asarray(np.repeat([-1.0, 1.0], half).reshape(QK_ROPE_DIM, 1), F32)
    sgn_row = jnp.asarray(np.tile(np.repeat([-1.0, 1.0], half), LANES // QK_ROPE_DIM).reshape(1, LANES), F32)
    const = lambda i: (0, 0)
    return pl.pallas_call(
        _rope_kernel,
        grid=(t // tn,),
        in_specs=[
            pl.BlockSpec((1, tn), lambda i: (0, i)),
            pl.BlockSpec((tn, 1), lambda i: (i, 0)),
            pl.BlockSpec((QK_ROPE_DIM, 1), const),
            pl.BlockSpec((1, LANES), const),
            pl.BlockSpec((QK_ROPE_DIM, 1), const),
            pl.BlockSpec((1, LANES), const),
        ],
        out_specs=[
            pl.BlockSpec((QK_ROPE_DIM, tn), lambda i: (0, i)),
            pl.BlockSpec((QK_ROPE_DIM, tn), lambda i: (0, i)),
            pl.BlockSpec((tn, LANES), lambda i: (i, 0)),
            pl.BlockSpec((tn, LANES), lambda i: (i, 0)),
        ],
        out_shape=[
            jax.ShapeDtypeStruct((QK_ROPE_DIM, t), F32),
            jax.ShapeDtypeStruct((QK_ROPE_DIM, t), F32),
            jax.ShapeDtypeStruct((t, LANES), F32),
            jax.ShapeDtypeStruct((t, LANES), F32),
        ],
        compiler_params=_cparams(("arbitrary",)),
        name="rope_tables",
    )(pos_f32.reshape(1, t), pos_f32.reshape(t, 1), invf_col, invf_row, sgn_col, sgn_row)


Q_SCALE = (QK_HEAD_DIM ** -0.5) * math.log2(math.e)


def _mix_in_kernel(x_ref, mod_ref, ng_ref, w_in_ref, w_kr_ref, cqg_ref, ckvg_ref, w_uq_ref, w_uv_ref, w_uk_ref,
                   qgn_ref, qgr_ref, qgs_ref, kgn_ref, kgr_ref, kgs_ref, cos4_ref, sin4_ref, cos_t_ref, sin_t_ref,
                   q_ref, kt_ref, v_ref, u_ref):
    d = x_ref.shape[1]
    x = x_ref[...]
    shift = mod_ref[:, 0:d]
    scale = mod_ref[:, d:2 * d]
    h = _rms(x, ng_ref[...]) * (1.0 + scale) + shift
    hb = h.astype(BF16)
    z = _dot(hb, w_in_ref[...])
    c_q = z[:, 0:Q_LORA_RANK]
    c_kv = z[:, Q_LORA_RANK:Q_LORA_RANK + KV_LORA_RANK]
    u_ref[...] = z[:, Q_LORA_RANK + KV_LORA_RANK:]
    cqn = _rms(c_q, cqg_ref[...]).astype(BF16)
    ckvn = _rms(c_kv, ckvg_ref[...]).astype(BF16)

    kn_t = lax.dot_general(w_uk_ref[...], ckvn, NT_DIMS, preferred_element_type=F32)
    kr_t = lax.dot_general(w_kr_ref[...], hb, NT_DIMS, preferred_element_type=F32)
    kr = kr_t[0:QK_ROPE_DIM]
    kr_rot = (kr * kgr_ref[...]) * cos_t_ref[...] + (kr_t[QK_ROPE_DIM:] * kgs_ref[...]) * sin_t_ref[...]
    ss_r = jnp.sum(kr * kr, axis=0, keepdims=True)
    zeros_r = jnp.zeros(kr_rot.shape, BF16)
    for hd in range(MLA_HEADS):
        kn = kn_t[hd * QK_NOPE_DIM:(hd + 1) * QK_NOPE_DIM]
        ss = jnp.sum(kn * kn, axis=0, keepdims=True) + ss_r
        r = lax.rsqrt(ss * (1.0 / QK_HEAD_DIM) + RMS_EPS)
        kt_ref[hd, 0:QK_NOPE_DIM, :] = ((kn * r) * kgn_ref[...]).astype(BF16)
        rope = (kr_rot * r).astype(BF16)
        lo, hi = (rope, zeros_r) if hd % 2 == 0 else (zeros_r, rope)
        kt_ref[hd, QK_NOPE_DIM:QK_NOPE_DIM + QK_ROPE_DIM, :] = lo
        kt_ref[hd, QK_NOPE_DIM + QK_ROPE_DIM:, :] = hi

    vz = _dot(ckvn, w_uv_ref[...])
    for hd in range(MLA_HEADS):
        v_ref[hd] = vz[:, hd * V_HEAD_DIM:(hd + 1) * V_HEAD_DIM].astype(BF16)

    qz = _dot(cqn, w_uq_ref[...])
    n_nope = MLA_HEADS * QK_NOPE_DIM
    n_rope = MLA_HEADS * QK_ROPE_DIM
    lane = lax.broadcasted_iota(jnp.int32, (x.shape[0], LANES), 1)
    for pair in range(MLA_HEADS // 2):
        pr = qz[:, n_nope + pair * LANES:n_nope + (pair + 1) * LANES]
        ps = qz[:, n_nope + n_rope + pair * LANES:n_nope + n_rope + (pair + 1) * LANES]
        rot = (pr * qgr_ref[...]) * cos4_ref[...] + (ps * qgs_ref[...]) * sin4_ref[...]
        pr2 = pr * pr
        for hd in (2 * pair, 2 * pair + 1):
            nope = qz[:, hd * QK_NOPE_DIM:(hd + 1) * QK_NOPE_DIM]
            own = (lane < QK_ROPE_DIM) if hd % 2 == 0 else (lane >= QK_ROPE_DIM)
            ss = jnp.sum(nope * nope + jnp.where(own, pr2, 0.0), axis=-1, keepdims=True)
            r = lax.rsqrt(ss * (1.0 / QK_HEAD_DIM) + RMS_EPS) * Q_SCALE
            q_ref[hd, :, 0:QK_NOPE_DIM] = ((nope * r) * qgn_ref[...]).astype(BF16)
            q_ref[hd, :, QK_NOPE_DIM:] = (rot * r).astype(BF16)


def _mix_in(x2d, mod_l, ng, wp, cos4, sin4, cos_t, sin_t):
    t, d = x2d.shape
    tm = min(MIX_TM, t)
    hh = MLA_HEADS
    const2 = lambda i: (0, 0)
    full = lambda a: pl.BlockSpec(a.shape, const2)
    ins = [x2d, mod_l, ng, wp["w_in"], wp["w_kr"], wp["cq_g"], wp["ckv_g"], wp["w_uq"], wp["w_uv"], wp["w_uk"],
           wp["qg_n"], wp["qg_r"], wp["qg_s"], wp["kg_n"], wp["kg_r"], wp["kg_s"]]
    in_specs = [pl.BlockSpec((tm, d), lambda i: (i, 0))] + [full(a) for a in ins[1:]]
    ins += [cos4, sin4, cos_t, sin_t]
    in_specs += [
        pl.BlockSpec((tm, LANES), lambda i: (i, 0)),
        pl.BlockSpec((tm, LANES), lambda i: (i, 0)),
        pl.BlockSpec((QK_ROPE_DIM, tm), lambda i: (0, i)),
        pl.BlockSpec((QK_ROPE_DIM, tm), lambda i: (0, i)),
    ]
    return pl.pallas_call(
        _mix_in_kernel,
        grid=(t // tm,),
        in_specs=in_specs,
        out_specs=[
            pl.BlockSpec((hh, tm, QK_PAD), lambda i: (0, i, 0)),
            pl.BlockSpec((hh, QK_PAD, tm), lambda i: (0, 0, i)),
            pl.BlockSpec((hh, tm, V_HEAD_DIM), lambda i: (0, i, 0)),
            pl.BlockSpec((tm, FNET_WIDTH), lambda i: (i, 0)),
        ],
        out_shape=[
            jax.ShapeDtypeStruct((hh, t, QK_PAD), BF16),
            jax.ShapeDtypeStruct((hh, QK_PAD, t), BF16),
            jax.ShapeDtypeStruct((hh, t, V_HEAD_DIM), BF16),
            jax.ShapeDtypeStruct((t, FNET_WIDTH), F32),
        ],
        compiler_params=_cparams(("arbitrary",)),
        name="mix_in",
    )(*ins)


def _attn_kernel(q_ref, kt_ref, v_ref, o_ref, m_ref, l_ref, acc_ref, *, tc):
    q = q_ref[0]
    m_ref[...] = jnp.full(m_ref.shape, -jnp.inf, F32)
    l_ref[...] = jnp.zeros(l_ref.shape, F32)
    acc_ref[...] = jnp.zeros(acc_ref.shape, F32)
    n_chunks = kt_ref.shape[2] // tc

    def body(c, carry):
        off = pl.multiple_of(c * tc, tc)
        s = _dot(q, kt_ref[0, :, pl.ds(off, tc)])
        m_prev = m_ref[...]
        m_new = jnp.maximum(m_prev, jnp.max(s, axis=-1, keepdims=True))
        p = jnp.exp2(s - m_new)
        alpha = jnp.exp2(m_prev - m_new)
        l_ref[...] = alpha * l_ref[...] + jnp.sum(p, axis=-1, keepdims=True)
        acc_ref[...] = alpha * acc_ref[...] + _dot(p.astype(BF16), v_ref[0, pl.ds(off, tc), :])
        m_ref[...] = m_new
        return carry

    lax.fori_loop(0, n_chunks, body, 0)
    o_ref[...] = (acc_ref[...] / l_ref[...]).astype(o_ref.dtype)


def _attention(q, kt, v):
    hh, t, _ = q.shape
    tq = min(ATT_TQ, t)
    tc = min(ATT_TC, t)
    return pl.pallas_call(
        functools.partial(_attn_kernel, tc=tc),
        grid=(hh, t // tq),
        in_specs=[
            pl.BlockSpec((1, tq, QK_PAD), lambda h, i: (h, i, 0)),
            pl.BlockSpec((1, QK_PAD, t), lambda h, i: (h, 0, 0)),
            pl.BlockSpec((1, t, V_HEAD_DIM), lambda h, i: (h, 0, 0)),
        ],
        out_specs=pl.BlockSpec((tq, V_HEAD_DIM), lambda h, i: (i, h)),
        out_shape=jax.ShapeDtypeStruct((t, hh * V_HEAD_DIM), BF16),
        scratch_shapes=[pltpu.VMEM((tq, 1), F32), pltpu.VMEM((tq, 1), F32), pltpu.VMEM((tq, V_HEAD_DIM), F32)],
        compiler_params=_cparams(("arbitrary", "arbitrary")),
        name="attention",
    )(q, kt, v)


def _fnet1_kernel(u_ref, f1_ref, twc_ref, tws_ref, zr_ref, zi_ref):
    k1, n2, width = zr_ref.shape
    y = _dot(f1_ref[...], u_ref[...].astype(BF16))
    a = y[0:n2]
    b = y[n2:]
    for i in range(k1):
        ai = a[:, i * width:(i + 1) * width]
        bi = b[:, i * width:(i + 1) * width]
        c = twc_ref[i]
        s = tws_ref[i]
        zr_ref[i] = (ai * c - bi * s).astype(zr_ref.dtype)
        zi_ref[i] = (-(bi * c) - ai * s).astype(zi_ref.dtype)


def _fnet2_kernel(zr_ref, zi_ref, f2_ref, cs_ref, wf_ref, o_ref):
    n1 = zr_ref.shape[0]
    k2 = zr_ref.shape[1] // FNET_WIDTH
    gd = FNET_GROUP_DIM
    zz = jnp.concatenate([zr_ref[...], zi_ref[...]], axis=0)
    x = _dot(f2_ref[...], zz)
    xr = x[0:n1]
    xi = x[n1:]
    for g in range(FNET_GROUPS):
        rows = []
        for j in range(k2):
            c0 = j * FNET_WIDTH + g * gd
            rows.append(jnp.concatenate([xr[:, c0:c0 + gd], xi[:, c0:c0 + gd]], axis=1))
        lhs = jnp.concatenate(rows, axis=0).astype(BF16)
        fg = _dot(lhs, cs_ref[...]).astype(BF16)
        og = _dot(fg, wf_ref[g])
        for j in range(k2):
            c0 = j * FNET_WIDTH + g * gd
            o_ref[:, c0:c0 + gd] = og[j * n1:(j + 1) * n1].astype(o_ref.dtype)


def _dft_consts(n1, n2):
    n = n1 * n2
    a2 = 2.0 * np.pi * np.outer(np.arange(n2), np.arange(n2)) / n2
    f1 = np.concatenate([np.cos(a2), np.sin(a2)], axis=0) / n2
    atw = 2.0 * np.pi * np.outer(np.arange(n1), np.arange(n2)) / n
    a1 = 2.0 * np.pi * np.outer(np.arange(n1), np.arange(n1)) / n1
    c1, s1 = np.cos(a1), np.sin(a1)
    f2 = np.block([[c1, s1], [-s1, c1]])
    ac = 2.0 * np.pi * np.outer(np.arange(FNET_GROUP_DIM), np.arange(FNET_GROUP_DIM)) / FNET_GROUP_DIM
    norm = n2 / math.sqrt(n * FNET_GROUP_DIM)
    cs = np.concatenate([np.cos(ac), np.sin(ac)], axis=0) * norm
    return (jnp.asarray(f1, BF16), jnp.asarray(np.cos(atw)[:, :, None], F32), jnp.asarray(np.sin(atw)[:, :, None], F32),
            jnp.asarray(f2, BF16), jnp.asarray(cs, BF16))


def _fnet(u, wf_b, n1, n2):
    t, width = u.shape
    assert t == n1 * n2
    f1, twc, tws, f2, cs = _dft_consts(n1, n2)
    k1 = min(FNET_K1, n1)
    k2 = min(FNET_K2, n2)
    const2 = lambda i: (0, 0)
    zr, zi = pl.pallas_call(
        _fnet1_kernel,
        grid=(n1 // k1,),
        in_specs=[
            pl.BlockSpec((n2, k1 * width), lambda i: (0, i)),
            pl.BlockSpec(f1.shape, const2),
            pl.BlockSpec((k1, n2, 1), lambda i: (i, 0, 0)),
            pl.BlockSpec((k1, n2, 1), lambda i: (i, 0, 0)),
        ],
        out_specs=[pl.BlockSpec((k1, n2, width), lambda i: (i, 0, 0))] * 2,
        out_shape=[jax.ShapeDtypeStruct((n1, n2, width), BF16)] * 2,
        compiler_params=_cparams(("arbitrary",)),
        name="fnet_stage1",
    )(u.reshape(n2, n1 * width), f1, twc, tws)
    out = pl.pallas_call(
        _fnet2_kernel,
        grid=(n2 // k2,),
        in_specs=[
            pl.BlockSpec((n1, k2 * width), lambda i: (0, i)),
            pl.BlockSpec((n1, k2 * width), lambda i: (0, i)),
            pl.BlockSpec(f2.shape, const2),
            pl.BlockSpec(cs.shape, const2),
            pl.BlockSpec(wf_b.shape, lambda i: (0, 0, 0)),
        ],
        out_specs=pl.BlockSpec((n1, k2 * width), lambda i: (0, i)),
        out_shape=jax.ShapeDtypeStruct((n1, n2 * width), BF16),
        compiler_params=_cparams(("arbitrary",)),
        name="fnet_stage2",
    )(zr.reshape(n1, n2 * width), zi.reshape(n1, n2 * width), f2, cs, wf_b)
    return out.reshape(t, width)


def _mix_out_kernel(a_ref, f_ref, x_ref, mod_ref, og_ref, w_out_ref, fg_ref, *rest, moe):
    d = x_ref.shape[1]
    half = a_ref.shape[1]
    an = _rms(a_ref[...].astype(F32), og_ref[:, 0:half])
    fn = _rms(f_ref[...].astype(F32), og_ref[:, half:])
    cat = jnp.concatenate([an, fn], axis=1).astype(BF16)
    o = _dot(cat, w_out_ref[...])
    gate1 = mod_ref[:, 2 * d:3 * d]
    shift2 = mod_ref[:, 3 * d:4 * d]
    scale2 = mod_ref[:, 4 * d:5 * d]
    x1 = x_ref[...] + gate1 * o
    h2 = _rms(x1, fg_ref[...]) * (1.0 + scale2) + shift2
    if not moe:
        x1_ref, h2_ref = rest
        x1_ref[...] = x1
        h2_ref[...] = h2.astype(h2_ref.dtype)
        return
    wr_ref, tri_ref, x1_ref, h2_ref, route_ref, cnt_ref, carry_ref = rest
    x1_ref[...] = x1
    h2_ref[...] = h2

    @pl.when(pl.program_id(0) == 0)
    def _():
        carry_ref[...] = jnp.zeros(carry_ref.shape, F32)

    tm = x1.shape[0]
    logits = jnp.dot(h2, wr_ref[...], preferred_element_type=F32, precision=lax.Precision.HIGHEST)
    lane = lax.broadcasted_iota(jnp.int32, (tm, LANES), 1)
    neg = jnp.float32(-jnp.inf)
    logits = jnp.where(lane < N_EXPERTS, logits, neg)
    m1 = jnp.max(logits, axis=-1, keepdims=True)
    i1 = jnp.min(jnp.where(logits == m1, lane, LANES), axis=-1, keepdims=True)
    rest_l = jnp.where(lane == i1, neg, logits)
    m2 = jnp.max(rest_l, axis=-1, keepdims=True)
    i2 = jnp.min(jnp.where(rest_l == m2, lane, LANES), axis=-1, keepdims=True)
    e21 = jnp.exp(m2 - m1)
    w1 = 1.0 / (1.0 + e21)
    w2 = e21 / (1.0 + e21)
    oh1 = jnp.where(lane == i1, 1.0, 0.0)
    oh2 = jnp.where(lane == i2, 1.0, 0.0)
    pre = _dot(tri_ref[...], jnp.concatenate([oh1, oh2], axis=1).astype(BF16))
    cnt1 = jnp.sum(oh1, axis=0, keepdims=True)
    cnt2 = jnp.sum(oh2, axis=0, keepdims=True)
    carry = carry_ref[0:1, :]
    rank1 = jnp.sum(oh1 * (carry + pre[:, 0:LANES]), axis=-1, keepdims=True)
    rank2 = jnp.sum(oh2 * (carry + cnt1 + pre[:, LANES:]), axis=-1, keepdims=True)
    new_carry = carry + cnt1 + cnt2
    carry_ref[...] = jnp.broadcast_to(new_carry, carry_ref.shape)
    cnt_ref[...] = jnp.broadcast_to(new_carry, cnt_ref.shape)
    vals = (i1.astype(F32), i2.astype(F32), w1, w2, rank1, rank2)
    route = jnp.zeros((tm, LANES), F32)
    for k, val in enumerate(vals):
        route = jnp.where(lane == k, val, route)
    route_ref[...] = route


def _mix_out(a, f, x2d, mod_l, og, w_out_b, fg, router=None):
    t, d = x2d.shape
    tm = min(MIX_TM, t)
    half = a.shape[1]
    moe = router is not None
    const2 = lambda i: (0, 0)
    row = lambda w: pl.BlockSpec((tm, w), lambda i: (i, 0))
    ins = [a, f, x2d, mod_l, og, w_out_b, fg]
    in_specs = [row(half), row(half), row(d), pl.BlockSpec(mod_l.shape, const2), pl.BlockSpec(og.shape, const2),
                pl.BlockSpec(w_out_b.shape, const2), pl.BlockSpec(fg.shape, const2)]
    out_specs = [row(d), row(d)]
    out_shape = [jax.ShapeDtypeStruct((t, d), F32), jax.ShapeDtypeStruct((t, d), F32 if moe else BF16)]
    scratch = []
    if moe:
        tri = jnp.asarray(np.tril(np.ones((tm, tm), np.float32), -1), BF16)
        ins += [router, tri]
        in_specs += [pl.BlockSpec(router.shape, const2), pl.BlockSpec(tri.shape, const2)]
        out_specs += [row(LANES), pl.BlockSpec((8, LANES), const2)]
        out_shape += [jax.ShapeDtypeStruct((t, LANES), F32), jax.ShapeDtypeStruct((8, LANES), F32)]
        scratch = [pltpu.VMEM((8, LANES), F32)]
    return pl.pallas_call(
        functools.partial(_mix_out_kernel, moe=moe),
        grid=(t // tm,),
        in_specs=in_specs,
        out_specs=out_specs,
        out_shape=out_shape,
        scratch_shapes=scratch,
        compiler_params=_cparams(("arbitrary",)),
        name="mix_out_moe" if moe else "mix_out",
    )(*ins)


def _silu(g):
    return g * (1.0 / (1.0 + jnp.exp(-g)))


def _ffn_kernel(h_ref, x1_ref, mod_ref, wg_ref, wu_ref, wd_ref, o_ref):
    j = pl.program_id(1)
    d = x1_ref.shape[1]
    h = h_ref[...]
    act = (_silu(_dot(h, wg_ref[...])) * _dot(h, wu_ref[...])).astype(BF16)
    part = _dot(act, wd_ref[...])

    @pl.when(j == 0)
    def _():
        o_ref[...] = part

    @pl.when(j > 0)
    def _():
        o_ref[...] += part

    @pl.when(j == pl.num_programs(1) - 1)
    def _():
        o_ref[...] = x1_ref[...] + mod_ref[:, 5 * d:6 * d] * o_ref[...]


def _ffn_dense(h2, x1, mod_l, wg, wu, wd):
    t, d = x1.shape
    ff = wg.shape[1]
    tm = min(FFN_TM, t)
    tf = min(FFN_TF, ff)
    return pl.pallas_call(
        _ffn_kernel,
        grid=(t // tm, ff // tf),
        in_specs=[
            pl.BlockSpec((tm, d), lambda i, j: (i, 0)),
            pl.BlockSpec((tm, d), lambda i, j: (i, 0)),
            pl.BlockSpec(mod_l.shape, lambda i, j: (0, 0)),
            pl.BlockSpec((d, tf), lambda i, j: (0, j)),
            pl.BlockSpec((d, tf), lambda i, j: (0, j)),
            pl.BlockSpec((tf, d), lambda i, j: (j, 0)),
        ],
        out_specs=pl.BlockSpec((tm, d), lambda i, j: (i, 0)),
        out_shape=jax.ShapeDtypeStruct((t, d), F32),
        compiler_params=_cparams(("arbitrary", "arbitrary")),
        name="ffn_dense",
    )(h2, x1, mod_l, wg, wu, wd)


def _dispatch_kernel(d1_ref, d2_ref, h_ref, xs_in_ref, xs_ref, sem):
    del xs_in_ref
    tm = h_ref.shape[0]
    base = pl.program_id(0) * tm

    def copy(r, dst):
        return pltpu.make_async_copy(h_ref.at[pl.ds(r, 1), :], xs_ref.at[pl.ds(dst, 1), :], sem)

    def start(r, c):
        copy(r, d1_ref[base + r]).start()
        copy(r, d2_ref[base + r]).start()
        return c

    def wait(r, c):
        copy(r, d1_ref[base + r]).wait()
        copy(r, d2_ref[base + r]).wait()
        return c

    lax.fori_loop(0, tm, start, 0)
    lax.fori_loop(0, tm, wait, 0)


def _dispatch(dest1, dest2, h2, n_rows):
    t, d = h2.shape
    tm = min(ROW_TM, t)
    xs0 = jnp.zeros((n_rows, d), h2.dtype)
    return pl.pallas_call(
        _dispatch_kernel,
        grid_spec=pltpu.PrefetchScalarGridSpec(
            num_scalar_prefetch=2,
            grid=(t // tm,),
            in_specs=[pl.BlockSpec((tm, d), lambda i, d1, d2: (i, 0)), pl.BlockSpec(memory_space=pl.ANY)],
            out_specs=pl.BlockSpec(memory_space=pl.ANY),
            scratch_shapes=[pltpu.SemaphoreType.DMA(())],
        ),
        out_shape=jax.ShapeDtypeStruct((n_rows, d), h2.dtype),
        input_output_aliases={3: 0},
        compiler_params=_cparams(("arbitrary",)),
        name="moe_dispatch",
    )(dest1, dest2, h2, xs0)


def _expert_kernel(be_ref, nu_ref, x_ref, wg_ref, wu_ref, wd_ref, o_ref, xb_ref):
    b = pl.program_id(0)
    j = pl.program_id(1)

    @pl.when(b < nu_ref[0])
    def _():
        @pl.when(j == 0)
        def _():
            xb_ref[...] = x_ref[...].astype(BF16)

        xb = xb_ref[...]
        act = (_silu(_dot(xb, wg_ref[0])) * _dot(xb, wu_ref[0])).astype(BF16)
        part = _dot(act, wd_ref[0])

        @pl.when(j == 0)
        def _():
            o_ref[...] = part

        @pl.when(j > 0)
        def _():
            o_ref[...] += part

    @pl.when(jnp.logical_and(b >= nu_ref[0], j == 0))
    def _():
        o_ref[...] = jnp.zeros(o_ref.shape, o_ref.dtype)


def _experts(block_e, n_used, xs, wg, wu, wd):
    n_rows, d = xs.shape
    ne, _, ff = wg.shape
    blk = min(MOE_BLK, n_rows)
    tf = min(MOE_TF, ff)
    nb = n_rows // blk
    nj = ff // tf

    def row_map(b, j, be, nu):
        return (jnp.minimum(b, nu[0] - 1), 0)

    def col_map(b, j, be, nu):
        live = b < nu[0]
        return (be[jnp.minimum(b, nu[0] - 1)], 0, jnp.where(live, j, nj - 1))

    def down_map(b, j, be, nu):
        live = b < nu[0]
        return (be[jnp.minimum(b, nu[0] - 1)], jnp.where(live, j, nj - 1), 0)

    return pl.pallas_call(
        _expert_kernel,
        grid_spec=pltpu.PrefetchScalarGridSpec(
            num_scalar_prefetch=2,
            grid=(nb, nj),
            in_specs=[
                pl.BlockSpec((blk, d), row_map),
                pl.BlockSpec((1, d, tf), col_map),
                pl.BlockSpec((1, d, tf), col_map),
                pl.BlockSpec((1, tf, d), down_map),
            ],
            out_specs=pl.BlockSpec((blk, d), lambda b, j, be, nu: (b, 0)),
            scratch_shapes=[pltpu.VMEM((blk, d), BF16)],
        ),
        out_shape=jax.ShapeDtypeStruct((n_rows, d), F32),
        compiler_params=_cparams(("arbitrary", "arbitrary")),
        name="moe_experts",
    )(block_e, n_used, xs, wg, wu, wd)


def _combine_kernel(d1_ref, d2_ref, yb_ref, route_ref, x1_ref, mod_ref, o_ref, buf_ref, sem):
    tm, d = x1_ref.shape
    base = pl.program_id(0) * tm

    def copy(slot, r, src):
        return pltpu.make_async_copy(yb_ref.at[pl.ds(src, 1), :], buf_ref.at[slot, pl.ds(r, 1), :], sem)

    def start(r, c):
        copy(0, r, d1_ref[base + r]).start()
        copy(1, r, d2_ref[base + r]).start()
        return c

    def wait(r, c):
        copy(0, r, d1_ref[base + r]).wait()
        copy(1, r, d2_ref[base + r]).wait()
        return c

    lax.fori_loop(0, tm, start, 0)
    lax.fori_loop(0, tm, wait, 0)
    w1 = route_ref[:, 2:3]
    w2 = route_ref[:, 3:4]
    y = buf_ref[0] * w1 + buf_ref[1] * w2
    o_ref[...] = x1_ref[...] + mod_ref[:, 5 * d:6 * d] * y


def _combine(dest1, dest2, yb, route, x1, mod_l):
    t, d = x1.shape
    tm = min(ROW_TM, t)
    row = lambda w: pl.BlockSpec((tm, w), lambda i, d1, d2: (i, 0))
    return pl.pallas_call(
        _combine_kernel,
        grid_spec=pltpu.PrefetchScalarGridSpec(
            num_scalar_prefetch=2,
            grid=(t // tm,),
            in_specs=[pl.BlockSpec(memory_space=pl.ANY), row(LANES), row(d),
                      pl.BlockSpec(mod_l.shape, lambda i, d1, d2: (0, 0))],
            out_specs=row(d),
            scratch_shapes=[pltpu.VMEM((2, tm, d), F32), pltpu.SemaphoreType.DMA(())],
        ),
        out_shape=jax.ShapeDtypeStruct((t, d), F32),
        compiler_params=_cparams(("arbitrary",)),
        name="moe_combine",
    )(dest1, dest2, yb, route, x1, mod_l)


def _moe_plan(route, cnt, t, blk):
    e1 = route[:, 0].astype(jnp.int32)
    e2 = route[:, 1].astype(jnp.int32)
    r1 = route[:, 4].astype(jnp.int32)
    r2 = route[:, 5].astype(jnp.int32)
    counts = cnt[0, :N_EXPERTS].astype(jnp.int32)
    padded = (counts + blk - 1) // blk * blk
    pend = jnp.cumsum(padded)
    pstart = pend - padded
    eids = jnp.arange(N_EXPERTS, dtype=jnp.int32)
    dest1 = jnp.sum(jnp.where(e1[:, None] == eids[None, :], pstart[None, :], 0), axis=1) + r1
    dest2 = jnp.sum(jnp.where(e2[:, None] == eids[None, :], pstart[None, :], 0), axis=1) + r2
    n_blocks = -(-(2 * t) // blk) + N_EXPERTS
    bstart = jnp.arange(n_blocks, dtype=jnp.int32) * blk
    block_e = jnp.minimum(jnp.sum((pend[None, :] <= bstart[:, None]).astype(jnp.int32), axis=1), N_EXPERTS - 1)
    n_used = (pend[-1] // blk).astype(jnp.int32).reshape(1)
    return dest1.astype(jnp.int32), dest2.astype(jnp.int32), block_e.astype(jnp.int32), n_used, n_blocks * blk


def _prep_mixer_weights(w_in, cq_norm, ckv_norm, w_uq, w_ukv, q_norm, k_norm):
    s1 = Q_LORA_RANK
    s2 = s1 + KV_LORA_RANK
    s3 = s2 + QK_ROPE_DIM
    half = QK_ROPE_DIM // 2
    swap = np.concatenate([np.arange(half, QK_ROPE_DIM), np.arange(half)])
    w_kr = w_in[:, s2:s3]
    uq = w_uq.reshape(Q_LORA_RANK, MLA_HEADS, QK_HEAD_DIM)
    uq_rope = uq[:, :, QK_NOPE_DIM:]
    ukv = w_ukv.reshape(KV_LORA_RANK, MLA_HEADS, QK_NOPE_DIM + V_HEAD_DIM)
    qg_r = q_norm[QK_NOPE_DIM:]
    kg_r = k_norm[QK_NOPE_DIM:]
    reps = LANES // QK_ROPE_DIM
    return {
        "w_in": jnp.concatenate([w_in[:, :s2], w_in[:, s3:]], axis=1).astype(BF16),
        "w_kr": jnp.concatenate([w_kr, w_kr[:, swap]], axis=1).T.astype(BF16),
        "cq_g": cq_norm.reshape(1, -1),
        "ckv_g": ckv_norm.reshape(1, -1),
        "w_uq": jnp.concatenate([uq[:, :, :QK_NOPE_DIM].reshape(Q_LORA_RANK, -1),
                                 uq_rope.reshape(Q_LORA_RANK, -1),
                                 uq_rope[:, :, swap].reshape(Q_LORA_RANK, -1)], axis=1).astype(BF16),
        "w_uv": ukv[:, :, QK_NOPE_DIM:].reshape(KV_LORA_RANK, -1).astype(BF16),
        "w_uk": ukv[:, :, :QK_NOPE_DIM].reshape(KV_LORA_RANK, -1).T.astype(BF16),
        "qg_n": q_norm[:QK_NOPE_DIM].reshape(1, -1),
        "qg_r": jnp.tile(qg_r, reps).reshape(1, -1),
        "qg_s": jnp.tile(qg_r[swap], reps).reshape(1, -1),
        "kg_n": k_norm[:QK_NOPE_DIM].reshape(-1, 1),
        "kg_r": kg_r.reshape(-1, 1),
        "kg_s": kg_r[swap].reshape(-1, 1),
    }


def _fnet_factors(t):
    n2 = 1 << (int(math.log2(t)) // 2)
    return t // n2, n2


def kernel(x, c, positions, ada_w, ada_b, mix_norm, w_in, cq_norm, ckv_norm, w_uq, w_ukv, q_norm, k_norm, w_fnet,
           out_norm, w_out, ffn_norm, dense_w_gate, dense_w_up, dense_w_down, router_w, moe_w_gate, moe_w_up,
           moe_w_down):
    b, s, d = x.shape
    assert b == 1
    t = b * s
    depth = ada_w.shape[0]
    x2d = x.reshape(t, d)
    mod = _ada_mod(c.reshape(d, 1), ada_w, ada_b.reshape(depth, 1, -1))
    cos_t, sin_t, cos4, sin4 = _rope_tables(positions.reshape(t).astype(F32))
    n1, n2 = _fnet_factors(t)
    for l in range(depth):
        mod_l = mod[l]
        wp = _prep_mixer_weights(w_in[l], cq_norm[l], ckv_norm[l], w_uq[l], w_ukv[l], q_norm[l], k_norm[l])
        q, kt, v, u = _mix_in(x2d, mod_l, mix_norm[l].reshape(1, d), wp, cos4, sin4, cos_t, sin_t)
        a = _attention(q, kt, v)
        f = _fnet(u, w_fnet[l].astype(BF16), n1, n2)
        og = out_norm[l].reshape(1, d)
        fg = ffn_norm[l].reshape(1, d)
        w_out_b = w_out[l].astype(BF16)
        j = l // 2
        if l % 2 == 0:
            x1, h2 = _mix_out(a, f, x2d, mod_l, og, w_out_b, fg)
            x2d = _ffn_dense(h2, x1, mod_l, dense_w_gate[j].astype(BF16), dense_w_up[j].astype(BF16),
                             dense_w_down[j].astype(BF16))
        else:
            router = jnp.pad(router_w[j], ((0, 0), (0, LANES - N_EXPERTS)))
            x1, h2, route, cnt = _mix_out(a, f, x2d, mod_l, og, w_out_b, fg, router=router)
            dest1, dest2, block_e, n_used, n_rows = _moe_plan(route, cnt, t, MOE_BLK)
            xs = _dispatch(dest1, dest2, h2, n_rows)
            yb = _experts(block_e, n_used, xs, moe_w_gate[j].astype(BF16), moe_w_up[j].astype(BF16),
                          moe_w_down[j].astype(BF16))
            x2d = _combine(dest1, dest2, yb, route, x1, mod_l)
    return x2d.reshape(b, s, d)
```

```python
import functools
import math

import numpy as np
import jax
import jax.numpy as jnp
from jax import lax
from jax.experimental import pallas as pl
from jax.experimental.pallas import tpu as pltpu

D_MODEL = 2048
DEPTH = 2
MLA_HEADS = 8
QK_NOPE_DIM = 128
QK_ROPE_DIM = 64
V_HEAD_DIM = 128
QK_HEAD_DIM = QK_NOPE_DIM + QK_ROPE_DIM
Q_LORA_RANK = 512
KV_LORA_RANK = 256
MLA_WIDTH = MLA_HEADS * V_HEAD_DIM
FNET_GROUPS = 8
FNET_GROUP_DIM = 128
FNET_WIDTH = FNET_GROUPS * FNET_GROUP_DIM
ROPE_THETA = 10000.0
N_EXPERTS = 8
N_MOD = 6
RMS_EPS = 1e-6

LANES = 128
QK_PAD = 256
VMEM_LIMIT_BYTES = 56 * 1024 * 1024

ADA_TN = 1024
ROPE_TN = 2048
MIX_TM = 256
ATT_TQ = 1024
ATT_TC = 1024
FNET_K1 = 4
FNET_K2 = 4
FFN_TM = 512
FFN_TF = 512
MOE_BLK = 512
MOE_TF = 512
ROW_TM = 256

BF16 = jnp.bfloat16
F32 = jnp.float32
NT_DIMS = (((1,), (1,)), ((), ()))


def _cparams(sem, vmem=VMEM_LIMIT_BYTES):
    return pltpu.CompilerParams(dimension_semantics=sem, vmem_limit_bytes=vmem)


def _dot(a, b):
    return jnp.dot(a, b, preferred_element_type=F32)


def _rms(x, g):
    ms = jnp.mean(x * x, axis=-1, keepdims=True)
    return (x * lax.rsqrt(ms + RMS_EPS)) * g


def _ada_kernel(c_ref, w_ref, b_ref, o_ref, cond_ref):
    c = c_ref[...]
    cond_ref[...] = c * (1.0 / (1.0 + jnp.exp(-c)))
    d = w_ref.shape[1]
    tn = w_ref.shape[2]
    rows = 64

    def body(i, acc):
        r = pl.multiple_of(i * rows, rows)
        prod = w_ref[0, pl.ds(r, rows), :] * cond_ref[pl.ds(r, rows), :]
        for s in range(rows // 8):
            acc = acc + prod[8 * s:8 * s + 8]
        return acc

    acc = lax.fori_loop(0, d // rows, body, jnp.zeros((8, tn), F32))
    o_ref[0] = jnp.sum(acc, axis=0, keepdims=True) + b_ref[0]


def _ada_mod(c_col, ada_w, ada_b3):
    depth, d, n = ada_w.shape
    tn = min(ADA_TN, n)
    return pl.pallas_call(
        _ada_kernel,
        grid=(depth, n // tn),
        in_specs=[
            pl.BlockSpec((d, 1), lambda l, j: (0, 0)),
            pl.BlockSpec((1, d, tn), lambda l, j: (l, 0, j)),
            pl.BlockSpec((1, 1, tn), lambda l, j: (l, 0, j)),
        ],
        out_specs=pl.BlockSpec((1, 1, tn), lambda l, j: (l, 0, j)),
        out_shape=jax.ShapeDtypeStruct((depth, 1, n), F32),
        scratch_shapes=[pltpu.VMEM((d, 1), F32)],
        compiler_params=_cparams(("arbitrary", "arbitrary")),
        name="ada_mod",
    )(c_col, ada_w, ada_b3)


def _rope_kernel(pos_row_ref, pos_col_ref, invf_col_ref, invf_row_ref, sgn_col_ref, sgn_row_ref,
                 cos_t_ref, sin_t_ref, cos4_ref, sin4_ref):
    ang_t = invf_col_ref[...] * pos_row_ref[...]
    cos_t_ref[...] = jnp.cos(ang_t)
    sin_t_ref[...] = jnp.sin(ang_t) * sgn_col_ref[...]
    ang4 = pos_col_ref[...] * invf_row_ref[...]
    cos4_ref[...] = jnp.cos(ang4)
    sin4_ref[...] = jnp.sin(ang4) * sgn_row_ref[...]


def _rope_tables(pos_f32):
    t = pos_f32.shape[0]
    tn = min(ROPE_TN, t)
    half = QK_ROPE_DIM // 2
    inv_freq = ROPE_THETA ** (-jnp.arange(half, dtype=F32) / half)
    invf_col = jnp.tile(inv_freq, 2).reshape(QK_ROPE_DIM, 1)
    invf_row = jnp.tile(inv_freq, LANES // half).reshape(1, LANES)
    sgn_col = jnp.asarray(np.repeat([-1.0, 1.0], half).reshape(QK_ROPE_DIM, 1), F32)
    sgn_row = jnp.asarray(np.tile(np.repeat([-1.0, 1.0], half), LANES // QK_ROPE_DIM).reshape(1, LANES), F32)
    const = lambda i: (0, 0)
    return pl.pallas_call(
        _rope_kernel,
        grid=(t // tn,),
        in_specs=[
            pl.BlockSpec((1, tn), lambda i: (0, i)),
            pl.BlockSpec((tn, 1), lambda i: (i, 0)),
            pl.BlockSpec((QK_ROPE_DIM, 1), const),
            pl.BlockSpec((1, LANES), const),
            pl.BlockSpec((QK_ROPE_DIM, 1), const),
            pl.BlockSpec((1, LANES), const),
        ],
        out_specs=[
            pl.BlockSpec((QK_ROPE_DIM, tn), lambda i: (0, i)),
            pl.BlockSpec((QK_ROPE_DIM, tn), lambda i: (0, i)),
            pl.BlockSpec((tn, LANES), lambda i: (i, 0)),
            pl.BlockSpec((tn, LANES), lambda i: (i, 0)),
        ],
        out_shape=[
            jax.ShapeDtypeStruct((QK_ROPE_DIM, t), F32),
            jax.ShapeDtypeStruct((QK_ROPE_DIM, t), F32),
            jax.ShapeDtypeStruct((t, LANES), F32),
            jax.ShapeDtypeStruct((t, LANES), F32),
        ],
        compiler_params=_cparams(("arbitrary",)),
        name="rope_tables",
    )(pos_f32.reshape(1, t), pos_f32.reshape(t, 1), invf_col, invf_row, sgn_col, sgn_row)


Q_SCALE = (QK_HEAD_DIM ** -0.5) * math.log2(math.e)


def _mix_in_kernel(x_ref, mod_ref, ng_ref, w_in_ref, w_kr_ref, cqg_ref, ckvg_ref, w_uq_ref, w_uv_ref, w_uk_ref,
                   qgn_ref, qgr_ref, qgs_ref, kgn_ref, kgr_ref, kgs_ref, cos4_ref, sin4_ref, cos_t_ref, sin_t_ref,
                   q_ref, kt_ref, v_ref, u_ref, kmax_ref):
    d = x_ref.shape[1]
    x = x_ref[...]
    shift = mod_ref[:, 0:d]
    scale = mod_ref[:, d:2 * d]
    h = _rms(x, ng_ref[...]) * (1.0 + scale) + shift
    hb = h.astype(BF16)
    z = _dot(hb, w_in_ref[...])
    c_q = z[:, 0:Q_LORA_RANK]
    c_kv = z[:, Q_LORA_RANK:Q_LORA_RANK + KV_LORA_RANK]
    u_ref[...] = z[:, Q_LORA_RANK + KV_LORA_RANK:]
    cqn = _rms(c_q, cqg_ref[...]).astype(BF16)
    ckvn = _rms(c_kv, ckvg_ref[...]).astype(BF16)

    kn_t = lax.dot_general(w_uk_ref[...], ckvn, NT_DIMS, preferred_element_type=F32)
    kr_t = lax.dot_general(w_kr_ref[...], hb, NT_DIMS, preferred_element_type=F32)
    kr = kr_t[0:QK_ROPE_DIM]
    kr_rot = (kr * kgr_ref[...]) * cos_t_ref[...] + (kr_t[QK_ROPE_DIM:] * kgs_ref[...]) * sin_t_ref[...]
    ss_r = jnp.sum(kr * kr, axis=0, keepdims=True)
    tm = x.shape[0]
    pad_rows = QK_PAD - QK_HEAD_DIM
    row = lax.broadcasted_iota(jnp.int32, (pad_rows, tm), 0)
    bias_rows = jnp.where(row == 0, 1.0, 0.0).astype(BF16)

    @pl.when(pl.program_id(0) == 0)
    def _():
        kmax_ref[...] = jnp.zeros(kmax_ref.shape, F32)

    for hd in range(MLA_HEADS):
        kn = kn_t[hd * QK_NOPE_DIM:(hd + 1) * QK_NOPE_DIM]
        ss = jnp.sum(kn * kn, axis=0, keepdims=True) + ss_r
        r = lax.rsqrt(ss * (1.0 / QK_HEAD_DIM) + RMS_EPS)
        kn_o = (kn * r) * kgn_ref[...]
        kr_o = kr_rot * r
        kt_ref[hd, 0:QK_NOPE_DIM, :] = kn_o.astype(BF16)
        kt_ref[hd, QK_NOPE_DIM:QK_HEAD_DIM, :] = kr_o.astype(BF16)
        kt_ref[hd, QK_HEAD_DIM:, :] = bias_rows
        norm2 = jnp.sum(kn_o * kn_o, axis=0, keepdims=True) + jnp.sum(kr_o * kr_o, axis=0, keepdims=True)
        kmax_ref[hd:hd + 1, :] = jnp.maximum(kmax_ref[hd:hd + 1, :], jnp.max(norm2, axis=-1, keepdims=True))

    vz = _dot(ckvn, w_uv_ref[...])
    for hd in range(MLA_HEADS):
        v_ref[hd] = vz[:, hd * V_HEAD_DIM:(hd + 1) * V_HEAD_DIM].astype(BF16)

    qz = _dot(cqn, w_uq_ref[...])
    n_nope = MLA_HEADS * QK_NOPE_DIM
    n_rope = MLA_HEADS * QK_ROPE_DIM
    lane = lax.broadcasted_iota(jnp.int32, (x.shape[0], LANES), 1)
    for pair in range(MLA_HEADS // 2):
        pr = qz[:, n_nope + pair * LANES:n_nope + (pair + 1) * LANES]
        ps = qz[:, n_nope + n_rope + pair * LANES:n_nope + n_rope + (pair + 1) * LANES]
        rot = (pr * qgr_ref[...]) * cos4_ref[...] + (ps * qgs_ref[...]) * sin4_ref[...]
        pr2 = pr * pr
        for hd in (2 * pair, 2 * pair + 1):
            nope = qz[:, hd * QK_NOPE_DIM:(hd + 1) * QK_NOPE_DIM]
            own = (lane < QK_ROPE_DIM) if hd % 2 == 0 else (lane >= QK_ROPE_DIM)
            ss = jnp.sum(nope * nope + jnp.where(own, pr2, 0.0), axis=-1, keepdims=True)
            r = lax.rsqrt(ss * (1.0 / QK_HEAD_DIM) + RMS_EPS) * Q_SCALE
            q_ref[hd, :, 0:QK_NOPE_DIM] = ((nope * r) * qgn_ref[...]).astype(BF16)
            rope = rot * r
            if hd % 2 == 1:
                rope = pltpu.roll(rope, QK_ROPE_DIM, 1)
            q_ref[hd, :, QK_NOPE_DIM:] = jnp.where(lane < QK_ROPE_DIM, rope, 0.0).astype(BF16)


def _mix_in(x2d, mod_l, ng, wp, cos4, sin4, cos_t, sin_t):
    t, d = x2d.shape
    tm = min(MIX_TM, t)
    hh = MLA_HEADS
    const2 = lambda i: (0, 0)
    full = lambda a: pl.BlockSpec(a.shape, const2)
    ins = [x2d, mod_l, ng, wp["w_in"], wp["w_kr"], wp["cq_g"], wp["ckv_g"], wp["w_uq"], wp["w_uv"], wp["w_uk"],
           wp["qg_n"], wp["qg_r"], wp["qg_s"], wp["kg_n"], wp["kg_r"], wp["kg_s"]]
    in_specs = [pl.BlockSpec((tm, d), lambda i: (i, 0))] + [full(a) for a in ins[1:]]
    ins += [cos4, sin4, cos_t, sin_t]
    in_specs += [
        pl.BlockSpec((tm, LANES), lambda i: (i, 0)),
        pl.BlockSpec((tm, LANES), lambda i: (i, 0)),
        pl.BlockSpec((QK_ROPE_DIM, tm), lambda i: (0, i)),
        pl.BlockSpec((QK_ROPE_DIM, tm), lambda i: (0, i)),
    ]
    return pl.pallas_call(
        _mix_in_kernel,
        grid=(t // tm,),
        in_specs=in_specs,
        out_specs=[
            pl.BlockSpec((hh, tm, QK_PAD), lambda i: (0, i, 0)),
            pl.BlockSpec((hh, QK_PAD, tm), lambda i: (0, 0, i)),
            pl.BlockSpec((hh, tm, V_HEAD_DIM), lambda i: (0, i, 0)),
            pl.BlockSpec((tm, FNET_WIDTH), lambda i: (i, 0)),
            pl.BlockSpec((hh, LANES), lambda i: (0, 0)),
        ],
        out_shape=[
            jax.ShapeDtypeStruct((hh, t, QK_PAD), BF16),
            jax.ShapeDtypeStruct((hh, QK_PAD, t), BF16),
            jax.ShapeDtypeStruct((hh, t, V_HEAD_DIM), BF16),
            jax.ShapeDtypeStruct((t, FNET_WIDTH), F32),
            jax.ShapeDtypeStruct((hh, LANES), F32),
        ],
        compiler_params=_cparams(("arbitrary",)),
        name="mix_in",
    )(*ins)


ATT_FAST_BOUND = 48.0


def _attn_kernel(q_ref, kt_ref, v_ref, kmax_ref, o_ref, qa_ref, m_ref, l_ref, acc_ref, lp_ref, *, tc):
    h = pl.program_id(0)
    tq = q_ref.shape[1]
    n_chunks = kt_ref.shape[2] // tc
    q_hi = q_ref[0, :, LANES:].astype(F32)
    q_lo = q_ref[0, :, 0:LANES].astype(F32)
    q_norm = jnp.sqrt(jnp.sum(q_lo * q_lo + q_hi * q_hi, axis=-1, keepdims=True))
    bound = q_norm * jnp.sqrt(kmax_ref[pl.ds(h, 1), :])
    lane = lax.broadcasted_iota(jnp.int32, (tq, LANES), 1)
    qa_ref[:, 0:LANES] = q_ref[0, :, 0:LANES]
    qa_ref[:, LANES:] = jnp.where(lane == QK_HEAD_DIM - LANES, -bound, q_hi).astype(BF16)
    fast = jnp.max(bound) <= ATT_FAST_BOUND
    acc_ref[...] = jnp.zeros(acc_ref.shape, F32)

    @pl.when(fast)
    def _():
        lp_ref[...] = jnp.zeros(lp_ref.shape, F32)

        def body(c, carry):
            off = pl.multiple_of(c * tc, tc)
            p = jnp.exp2(_dot(qa_ref[...], kt_ref[0, :, pl.ds(off, tc)]))
            part = p[:, 0:LANES]
            for k in range(1, tc // LANES):
                part = part + p[:, k * LANES:(k + 1) * LANES]
            lp_ref[...] += part
            acc_ref[...] += _dot(p.astype(BF16), v_ref[0, pl.ds(off, tc), :])
            return carry

        lax.fori_loop(0, n_chunks, body, 0)
        l_ref[...] = jnp.sum(lp_ref[...], axis=-1, keepdims=True)

    @pl.when(jnp.logical_not(fast))
    def _():
        m_ref[...] = jnp.full(m_ref.shape, -jnp.inf, F32)
        l_ref[...] = jnp.zeros(l_ref.shape, F32)

        def body(c, carry):
            off = pl.multiple_of(c * tc, tc)
            s = _dot(qa_ref[...], kt_ref[0, :, pl.ds(off, tc)])
            m_prev = m_ref[...]
            m_new = jnp.maximum(m_prev, jnp.max(s, axis=-1, keepdims=True))
            p = jnp.exp2(s - m_new)
            alpha = jnp.exp2(m_prev - m_new)
            l_ref[...] = alpha * l_ref[...] + jnp.sum(p, axis=-1, keepdims=True)
            acc_ref[...] = alpha * acc_ref[...] + _dot(p.astype(BF16), v_ref[0, pl.ds(off, tc), :])
            m_ref[...] = m_new
            return carry

        lax.fori_loop(0, n_chunks, body, 0)

    o_ref[...] = (acc_ref[...] / l_ref[...]).astype(o_ref.dtype)


def _attention(q, kt, v, kmax2):
    hh, t, _ = q.shape
    tq = min(ATT_TQ, t)
    tc = min(ATT_TC, t)
    return pl.pallas_call(
        functools.partial(_attn_kernel, tc=tc),
        grid=(hh, t // tq),
        in_specs=[
            pl.BlockSpec((1, tq, QK_PAD), lambda h, i: (h, i, 0)),
            pl.BlockSpec((1, QK_PAD, t), lambda h, i: (h, 0, 0)),
            pl.BlockSpec((1, t, V_HEAD_DIM), lambda h, i: (h, 0, 0)),
            pl.BlockSpec((hh, LANES), lambda h, i: (0, 0)),
        ],
        out_specs=pl.BlockSpec((tq, V_HEAD_DIM), lambda h, i: (i, h)),
        out_shape=jax.ShapeDtypeStruct((t, hh * V_HEAD_DIM), BF16),
        scratch_shapes=[pltpu.VMEM((tq, QK_PAD), BF16), pltpu.VMEM((tq, 1), F32), pltpu.VMEM((tq, 1), F32),
                        pltpu.VMEM((tq, V_HEAD_DIM), F32), pltpu.VMEM((tq, LANES), F32)],
        compiler_params=_cparams(("arbitrary", "arbitrary")),
        name="attention",
    )(q, kt, v, kmax2)


def _fnet1_kernel(u_ref, f1_ref, twc_ref, tws_ref, zr_ref, zi_ref):
    k1, n2, width = zr_ref.shape
    y = _dot(f1_ref[...], u_ref[...].astype(BF16))
    a = y[0:n2]
    b = y[n2:]
    for i in range(k1):
        ai = a[:, i * width:(i + 1) * width]
        bi = b[:, i * width:(i + 1) * width]
        c = twc_ref[i]
        s = tws_ref[i]
        zr_ref[i] = (ai * c - bi * s).astype(zr_ref.dtype)
        zi_ref[i] = (-(bi * c) - ai * s).astype(zi_ref.dtype)


def _fnet2_kernel(zr_ref, zi_ref, f2_ref, cs_ref, wf_ref, o_ref):
    n1 = zr_ref.shape[0]
    k2 = zr_ref.shape[1] // FNET_WIDTH
    gd = FNET_GROUP_DIM
    zz = jnp.concatenate([zr_ref[...], zi_ref[...]], axis=0)
    x = _dot(f2_ref[...], zz)
    xr = x[0:n1]
    xi = x[n1:]
    for g in range(FNET_GROUPS):
        rows = []
        for j in range(k2):
            c0 = j * FNET_WIDTH + g * gd
            rows.append(jnp.concatenate([xr[:, c0:c0 + gd], xi[:, c0:c0 + gd]], axis=1))
        lhs = jnp.concatenate(rows, axis=0).astype(BF16)
        fg = _dot(lhs, cs_ref[...]).astype(BF16)
        og = _dot(fg, wf_ref[g])
        for j in range(k2):
            c0 = j * FNET_WIDTH + g * gd
            o_ref[:, c0:c0 + gd] = og[j * n1:(j + 1) * n1].astype(o_ref.dtype)


def _dft_consts(n1, n2):
    n = n1 * n2
    a2 = 2.0 * np.pi * np.outer(np.arange(n2), np.arange(n2)) / n2
    f1 = np.concatenate([np.cos(a2), np.sin(a2)], axis=0) / n2
    atw = 2.0 * np.pi * np.outer(np.arange(n1), np.arange(n2)) / n
    a1 = 2.0 * np.pi * np.outer(np.arange(n1), np.arange(n1)) / n1
    c1, s1 = np.cos(a1), np.sin(a1)
    f2 = np.block([[c1, s1], [-s1, c1]])
    ac = 2.0 * np.pi * np.outer(np.arange(FNET_GROUP_DIM), np.arange(FNET_GROUP_DIM)) / FNET_GROUP_DIM
    norm = n2 / math.sqrt(n * FNET_GROUP_DIM)
    cs = np.concatenate([np.cos(ac), np.sin(ac)], axis=0) * norm
    return (jnp.asarray(f1, BF16), jnp.asarray(np.cos(atw)[:, :, None], F32), jnp.asarray(np.sin(atw)[:, :, None], F32),
            jnp.asarray(f2, BF16), jnp.asarray(cs, BF16))


def _fnet(u, wf_b, n1, n2):
    t, width = u.shape
    assert t == n1 * n2
    f1, twc, tws, f2, cs = _dft_consts(n1, n2)
    k1 = min(FNET_K1, n1)
    k2 = min(FNET_K2, n2)
    const2 = lambda i: (0, 0)
    zr, zi = pl.pallas_call(
        _fnet1_kernel,
        grid=(n1 // k1,),
        in_specs=[
            pl.BlockSpec((n2, k1 * width), lambda i: (0, i)),
            pl.BlockSpec(f1.shape, const2),
            pl.BlockSpec((k1, n2, 1), lambda i: (i, 0, 0)),
            pl.BlockSpec((k1, n2, 1), lambda i: (i, 0, 0)),
        ],
        out_specs=[pl.BlockSpec((k1, n2, width), lambda i: (i, 0, 0))] * 2,
        out_shape=[jax.ShapeDtypeStruct((n1, n2, width), BF16)] * 2,
        compiler_params=_cparams(("arbitrary",)),
        name="fnet_stage1",
    )(u.reshape(n2, n1 * width), f1, twc, tws)
    out = pl.pallas_call(
        _fnet2_kernel,
        grid=(n2 // k2,),
        in_specs=[
            pl.BlockSpec((n1, k2 * width), lambda i: (0, i)),
            pl.BlockSpec((n1, k2 * width), lambda i: (0, i)),
            pl.BlockSpec(f2.shape, const2),
            pl.BlockSpec(cs.shape, const2),
            pl.BlockSpec(wf_b.shape, lambda i: (0, 0, 0)),
        ],
        out_specs=pl.BlockSpec((n1, k2 * width), lambda i: (0, i)),
        out_shape=jax.ShapeDtypeStruct((n1, n2 * width), BF16),
        compiler_params=_cparams(("arbitrary",)),
        name="fnet_stage2",
    )(zr.reshape(n1, n2 * width), zi.reshape(n1, n2 * width), f2, cs, wf_b)
    return out.reshape(t, width)


def _mix_out_kernel(a_ref, f_ref, x_ref, mod_ref, og_ref, w_out_ref, fg_ref, *rest, moe):
    d = x_ref.shape[1]
    half = a_ref.shape[1]
    an = _rms(a_ref[...].astype(F32), og_ref[:, 0:half])
    fn = _rms(f_ref[...].astype(F32), og_ref[:, half:])
    cat = jnp.concatenate([an, fn], axis=1).astype(BF16)
    o = _dot(cat, w_out_ref[...])
    gate1 = mod_ref[:, 2 * d:3 * d]
    shift2 = mod_ref[:, 3 * d:4 * d]
    scale2 = mod_ref[:, 4 * d:5 * d]
    x1 = x_ref[...] + gate1 * o
    h2 = _rms(x1, fg_ref[...]) * (1.0 + scale2) + shift2
    if not moe:
        x1_ref, h2_ref = rest
        x1_ref[...] = x1
        h2_ref[...] = h2.astype(h2_ref.dtype)
        return
    wr_ref, tri_ref, x1_ref, h2_ref, route_ref, cnt_ref, carry_ref = rest
    x1_ref[...] = x1
    h2_ref[...] = h2

    @pl.when(pl.program_id(0) == 0)
    def _():
        carry_ref[...] = jnp.zeros(carry_ref.shape, F32)

    tm = x1.shape[0]
    h_hi = h2.astype(BF16)
    h_lo = (h2 - h_hi.astype(F32)).astype(BF16)
    hw = _dot(h_hi, wr_ref[...])
    logits = (hw[:, 0:LANES] + hw[:, LANES:]) + _dot(h_lo, wr_ref[:, 0:LANES])
    lane = lax.broadcasted_iota(jnp.int32, (tm, LANES), 1)
    neg = jnp.float32(-jnp.inf)
    logits = jnp.where(lane < N_EXPERTS, logits, neg)
    m1 = jnp.max(logits, axis=-1, keepdims=True)
    i1 = jnp.min(jnp.where(logits == m1, lane, LANES), axis=-1, keepdims=True)
    rest_l = jnp.where(lane == i1, neg, logits)
    m2 = jnp.max(rest_l, axis=-1, keepdims=True)
    i2 = jnp.min(jnp.where(rest_l == m2, lane, LANES), axis=-1, keepdims=True)
    e21 = jnp.exp(m2 - m1)
    w1 = 1.0 / (1.0 + e21)
    w2 = e21 / (1.0 + e21)
    oh1 = jnp.where(lane == i1, 1.0, 0.0)
    oh2 = jnp.where(lane == i2, 1.0, 0.0)
    pre = _dot(tri_ref[...], jnp.concatenate([oh1, oh2], axis=1).astype(BF16))
    cnt1 = jnp.sum(oh1, axis=0, keepdims=True)
    cnt2 = jnp.sum(oh2, axis=0, keepdims=True)
    carry = carry_ref[0:1, :]
    rank1 = jnp.sum(oh1 * (carry + pre[:, 0:LANES]), axis=-1, keepdims=True)
    rank2 = jnp.sum(oh2 * (carry + cnt1 + pre[:, LANES:]), axis=-1, keepdims=True)
    new_carry = carry + cnt1 + cnt2
    carry_ref[...] = jnp.broadcast_to(new_carry, carry_ref.shape)
    cnt_ref[...] = jnp.broadcast_to(new_carry, cnt_ref.shape)
    vals = (i1.astype(F32), i2.astype(F32), w1, w2, rank1, rank2)
    route = jnp.zeros((tm, LANES), F32)
    for k, val in enumerate(vals):
        route = jnp.where(lane == k, val, route)
    route_ref[...] = route


def _mix_out(a, f, x2d, mod_l, og, w_out_b, fg, router=None):
    t, d = x2d.shape
    tm = min(MIX_TM, t)
    half = a.shape[1]
    moe = router is not None
    const2 = lambda i: (0, 0)
    row = lambda w: pl.BlockSpec((tm, w), lambda i: (i, 0))
    ins = [a, f, x2d, mod_l, og, w_out_b, fg]
    in_specs = [row(half), row(half), row(d), pl.BlockSpec(mod_l.shape, const2), pl.BlockSpec(og.shape, const2),
                pl.BlockSpec(w_out_b.shape, const2), pl.BlockSpec(fg.shape, const2)]
    out_specs = [row(d), row(d)]
    out_shape = [jax.ShapeDtypeStruct((t, d), F32), jax.ShapeDtypeStruct((t, d), F32 if moe else BF16)]
    scratch = []
    if moe:
        tri = jnp.asarray(np.tril(np.ones((tm, tm), np.float32), -1), BF16)
        ins += [router, tri]
        in_specs += [pl.BlockSpec(router.shape, const2), pl.BlockSpec(tri.shape, const2)]
        out_specs += [row(LANES), pl.BlockSpec((8, LANES), const2)]
        out_shape += [jax.ShapeDtypeStruct((t, LANES), F32), jax.ShapeDtypeStruct((8, LANES), F32)]
        scratch = [pltpu.VMEM((8, LANES), F32)]
    return pl.pallas_call(
        functools.partial(_mix_out_kernel, moe=moe),
        grid=(t // tm,),
        in_specs=in_specs,
        out_specs=out_specs,
        out_shape=out_shape,
        scratch_shapes=scratch,
        compiler_params=_cparams(("arbitrary",)),
        name="mix_out_moe" if moe else "mix_out",
    )(*ins)


def _silu(g):
    return g * (1.0 / (1.0 + jnp.exp(-g)))


def _ffn_kernel(h_ref, x1_ref, mod_ref, wg_ref, wu_ref, wd_ref, o_ref):
    j = pl.program_id(1)
    d = x1_ref.shape[1]
    @pl.when(j == 0)
    def _():
        o_ref[...] = jnp.zeros(o_ref.shape, F32)

    h = h_ref[...]
    act = (_silu(_dot(h, wg_ref[...])) * _dot(h, wu_ref[...])).astype(BF16)
    o_ref[...] += _dot(act, wd_ref[...])

    @pl.when(j == pl.num_programs(1) - 1)
    def _():
        o_ref[...] = x1_ref[...] + mod_ref[:, 5 * d:6 * d] * o_ref[...]


def _ffn_dense(h2, x1, mod_l, wg, wu, wd):
    t, d = x1.shape
    ff = wg.shape[1]
    tm = min(FFN_TM, t)
    tf = min(FFN_TF, ff)
    return pl.pallas_call(
        _ffn_kernel,
        grid=(t // tm, ff // tf),
        in_specs=[
            pl.BlockSpec((tm, d), lambda i, j: (i, 0)),
            pl.BlockSpec((tm, d), lambda i, j: (i, 0)),
            pl.BlockSpec(mod_l.shape, lambda i, j: (0, 0)),
            pl.BlockSpec((d, tf), lambda i, j: (0, j)),
            pl.BlockSpec((d, tf), lambda i, j: (0, j)),
            pl.BlockSpec((tf, d), lambda i, j: (j, 0)),
        ],
        out_specs=pl.BlockSpec((tm, d), lambda i, j: (i, 0)),
        out_shape=jax.ShapeDtypeStruct((t, d), F32),
        compiler_params=_cparams(("arbitrary", "arbitrary")),
        name="ffn_dense",
    )(h2, x1, mod_l, wg, wu, wd)


def _dispatch_kernel(d1_ref, d2_ref, h_ref, xs_in_ref, xs_ref, sem):
    del xs_in_ref
    tm = h_ref.shape[0]
    base = pl.program_id(0) * tm

    def copy(r, dst):
        return pltpu.make_async_copy(h_ref.at[pl.ds(r, 1), :], xs_ref.at[pl.ds(dst, 1), :], sem)

    def start(r, c):
        copy(r, d1_ref[base + r]).start()
        copy(r, d2_ref[base + r]).start()
        return c

    def wait(r, c):
        copy(r, d1_ref[base + r]).wait()
        copy(r, d2_ref[base + r]).wait()
        return c

    lax.fori_loop(0, tm, start, 0)
    lax.fori_loop(0, tm, wait, 0)


def _dispatch(dest1, dest2, h2, n_rows):
    t, d = h2.shape
    tm = min(ROW_TM, t)
    xs0 = jnp.zeros((n_rows, d), h2.dtype)
    return pl.pallas_call(
        _dispatch_kernel,
        grid_spec=pltpu.PrefetchScalarGridSpec(
            num_scalar_prefetch=2,
            grid=(t // tm,),
            in_specs=[pl.BlockSpec((tm, d), lambda i, d1, d2: (i, 0)), pl.BlockSpec(memory_space=pl.ANY)],
            out_specs=pl.BlockSpec(memory_space=pl.ANY),
            scratch_shapes=[pltpu.SemaphoreType.DMA(())],
        ),
        out_shape=jax.ShapeDtypeStruct((n_rows, d), h2.dtype),
        input_output_aliases={3: 0},
        compiler_params=_cparams(("arbitrary",)),
        name="moe_dispatch",
    )(dest1, dest2, h2, xs0)


def _expert_kernel(be_ref, nu_ref, x_ref, wg_ref, wu_ref, wd_ref, o_ref, xb_ref):
    b = pl.program_id(0)
    j = pl.program_id(1)

    @pl.when(j == 0)
    def _():
        o_ref[...] = jnp.zeros(o_ref.shape, o_ref.dtype)
        xb_ref[...] = x_ref[...].astype(BF16)

    @pl.when(b < nu_ref[0])
    def _():
        xb = xb_ref[...]
        act = (_silu(_dot(xb, wg_ref[0])) * _dot(xb, wu_ref[0])).astype(BF16)
        o_ref[...] += _dot(act, wd_ref[0])


def _experts(block_e, n_used, xs, wg, wu, wd):
    n_rows, d = xs.shape
    ne, _, ff = wg.shape
    blk = min(MOE_BLK, n_rows)
    tf = min(MOE_TF, ff)
    nb = n_rows // blk
    nj = ff // tf

    def row_map(b, j, be, nu):
        return (jnp.minimum(b, nu[0] - 1), 0)

    def col_map(b, j, be, nu):
        live = b < nu[0]
        return (be[jnp.minimum(b, nu[0] - 1)], 0, jnp.where(live, j, nj - 1))

    def down_map(b, j, be, nu):
        live = b < nu[0]
        return (be[jnp.minimum(b, nu[0] - 1)], jnp.where(live, j, nj - 1), 0)

    return pl.pallas_call(
        _expert_kernel,
        grid_spec=pltpu.PrefetchScalarGridSpec(
            num_scalar_prefetch=2,
            grid=(nb, nj),
            in_specs=[
                pl.BlockSpec((blk, d), row_map),
                pl.BlockSpec((1, d, tf), col_map),
                pl.BlockSpec((1, d, tf), col_map),
                pl.BlockSpec((1, tf, d), down_map),
            ],
            out_specs=pl.BlockSpec((blk, d), lambda b, j, be, nu: (b, 0)),
            scratch_shapes=[pltpu.VMEM((blk, d), BF16)],
        ),
        out_shape=jax.ShapeDtypeStruct((n_rows, d), F32),
        compiler_params=_cparams(("arbitrary", "arbitrary")),
        name="moe_experts",
    )(block_e, n_used, xs, wg, wu, wd)


def _combine_kernel(d1_ref, d2_ref, yb_ref, route_ref, x1_ref, mod_ref, o_ref, buf_ref, sem):
    tm, d = x1_ref.shape
    base = pl.program_id(0) * tm

    def copy(slot, r, src):
        return pltpu.make_async_copy(yb_ref.at[pl.ds(src, 1), :], buf_ref.at[slot, pl.ds(r, 1), :], sem)

    def start(r, c):
        copy(0, r, d1_ref[base + r]).start()
        copy(1, r, d2_ref[base + r]).start()
        return c

    def wait(r, c):
        copy(0, r, d1_ref[base + r]).wait()
        copy(1, r, d2_ref[base + r]).wait()
        return c

    lax.fori_loop(0, tm, start, 0)
    lax.fori_loop(0, tm, wait, 0)
    w1 = route_ref[:, 2:3]
    w2 = route_ref[:, 3:4]
    y = buf_ref[0] * w1 + buf_ref[1] * w2
    o_ref[...] = x1_ref[...] + mod_ref[:, 5 * d:6 * d] * y


def _combine(dest1, dest2, yb, route, x1, mod_l):
    t, d = x1.shape
    tm = min(ROW_TM, t)
    row = lambda w: pl.BlockSpec((tm, w), lambda i, d1, d2: (i, 0))
    return pl.pallas_call(
        _combine_kernel,
        grid_spec=pltpu.PrefetchScalarGridSpec(
            num_scalar_prefetch=2,
            grid=(t // tm,),
            in_specs=[pl.BlockSpec(memory_space=pl.ANY), row(LANES), row(d),
                      pl.BlockSpec(mod_l.shape, lambda i, d1, d2: (0, 0))],
            out_specs=row(d),
            scratch_shapes=[pltpu.VMEM((2, tm, d), F32), pltpu.SemaphoreType.DMA(())],
        ),
        out_shape=jax.ShapeDtypeStruct((t, d), F32),
        compiler_params=_cparams(("arbitrary",)),
        name="moe_combine",
    )(dest1, dest2, yb, route, x1, mod_l)


def _moe_plan(route, cnt, t, blk):
    e1 = route[:, 0].astype(jnp.int32)
    e2 = route[:, 1].astype(jnp.int32)
    r1 = route[:, 4].astype(jnp.int32)
    r2 = route[:, 5].astype(jnp.int32)
    counts = cnt[0, :N_EXPERTS].astype(jnp.int32)
    padded = (counts + blk - 1) // blk * blk
    pend = jnp.cumsum(padded)
    pstart = pend - padded
    eids = jnp.arange(N_EXPERTS, dtype=jnp.int32)
    dest1 = jnp.sum(jnp.where(e1[:, None] == eids[None, :], pstart[None, :], 0), axis=1) + r1
    dest2 = jnp.sum(jnp.where(e2[:, None] == eids[None, :], pstart[None, :], 0), axis=1) + r2
    n_blocks = -(-(2 * t) // blk) + N_EXPERTS
    bstart = jnp.arange(n_blocks, dtype=jnp.int32) * blk
    block_e = jnp.minimum(jnp.sum((pend[None, :] <= bstart[:, None]).astype(jnp.int32), axis=1), N_EXPERTS - 1)
    n_used = (pend[-1] // blk).astype(jnp.int32).reshape(1)
    return dest1.astype(jnp.int32), dest2.astype(jnp.int32), block_e.astype(jnp.int32), n_used, n_blocks * blk


def _prep_mixer_weights(w_in, cq_norm, ckv_norm, w_uq, w_ukv, q_norm, k_norm):
    s1 = Q_LORA_RANK
    s2 = s1 + KV_LORA_RANK
    s3 = s2 + QK_ROPE_DIM
    half = QK_ROPE_DIM // 2
    swap = np.concatenate([np.arange(half, QK_ROPE_DIM), np.arange(half)])
    w_kr = w_in[:, s2:s3]
    uq = w_uq.reshape(Q_LORA_RANK, MLA_HEADS, QK_HEAD_DIM)
    uq_rope = uq[:, :, QK_NOPE_DIM:]
    ukv = w_ukv.reshape(KV_LORA_RANK, MLA_HEADS, QK_NOPE_DIM + V_HEAD_DIM)
    qg_r = q_norm[QK_NOPE_DIM:]
    kg_r = k_norm[QK_NOPE_DIM:]
    reps = LANES // QK_ROPE_DIM
    return {
        "w_in": jnp.concatenate([w_in[:, :s2], w_in[:, s3:]], axis=1).astype(BF16),
        "w_kr": jnp.concatenate([w_kr, w_kr[:, swap]], axis=1).T.astype(BF16),
        "cq_g": cq_norm.reshape(1, -1),
        "ckv_g": ckv_norm.reshape(1, -1),
        "w_uq": jnp.concatenate([uq[:, :, :QK_NOPE_DIM].reshape(Q_LORA_RANK, -1),
                                 uq_rope.reshape(Q_LORA_RANK, -1),
                                 uq_rope[:, :, swap].reshape(Q_LORA_RANK, -1)], axis=1).astype(BF16),
        "w_uv": ukv[:, :, QK_NOPE_DIM:].reshape(KV_LORA_RANK, -1).astype(BF16),
        "w_uk": ukv[:, :, :QK_NOPE_DIM].reshape(KV_LORA_RANK, -1).T.astype(BF16),
        "qg_n": q_norm[:QK_NOPE_DIM].reshape(1, -1),
        "qg_r": jnp.tile(qg_r, reps).reshape(1, -1),
        "qg_s": jnp.tile(qg_r[swap], reps).reshape(1, -1),
        "kg_n": k_norm[:QK_NOPE_DIM].reshape(-1, 1),
        "kg_r": kg_r.reshape(-1, 1),
        "kg_s": kg_r[swap].reshape(-1, 1),
    }


def _fnet_factors(t):
    n2 = 1 << (int(math.log2(t)) // 2)
    return t // n2, n2


def kernel(x, c, positions, ada_w, ada_b, mix_norm, w_in, cq_norm, ckv_norm, w_uq, w_ukv, q_norm, k_norm, w_fnet,
           out_norm, w_out, ffn_norm, dense_w_gate, dense_w_up, dense_w_down, router_w, moe_w_gate, moe_w_up,
           moe_w_down):
    b, s, d = x.shape
    assert b == 1
    t = b * s
    depth = ada_w.shape[0]
    x2d = x.reshape(t, d)
    mod = _ada_mod(c.reshape(d, 1), ada_w, ada_b.reshape(depth, 1, -1))
    cos_t, sin_t, cos4, sin4 = _rope_tables(positions.reshape(t).astype(F32))
    n1, n2 = _fnet_factors(t)
    for l in range(depth):
        mod_l = mod[l]
        wp = _prep_mixer_weights(w_in[l], cq_norm[l], ckv_norm[l], w_uq[l], w_ukv[l], q_norm[l], k_norm[l])
        q, kt, v, u, kmax2 = _mix_in(x2d, mod_l, mix_norm[l].reshape(1, d), wp, cos4, sin4, cos_t, sin_t)
        a = _attention(q, kt, v, kmax2)
        f = _fnet(u, w_fnet[l].astype(BF16), n1, n2)
        og = out_norm[l].reshape(1, d)
        fg = ffn_norm[l].reshape(1, d)
        w_out_b = w_out[l].astype(BF16)
        j = l // 2
        if l % 2 == 0:
            x1, h2 = _mix_out(a, f, x2d, mod_l, og, w_out_b, fg)
            x2d = _ffn_dense(h2, x1, mod_l, dense_w_gate[j].astype(BF16), dense_w_up[j].astype(BF16),
                             dense_w_down[j].astype(BF16))
        else:
            wr = jnp.pad(router_w[j], ((0, 0), (0, LANES - N_EXPERTS)))
            wr_hi = wr.astype(BF16)
            router = jnp.concatenate([wr_hi, (wr - wr_hi.astype(F32)).astype(BF16)], axis=1)
            x1, h2, route, cnt = _mix_out(a, f, x2d, mod_l, og, w_out_b, fg, router=router)
            dest1, dest2, block_e, n_used, n_rows = _moe_plan(route, cnt, t, MOE_BLK)
            xs = _dispatch(dest1, dest2, h2, n_rows)
            yb = _experts(block_e, n_used, xs, moe_w_gate[j].astype(BF16), moe_w_up[j].astype(BF16),
                          moe_w_down[j].astype(BF16))
            x2d = _combine(dest1, dest2, yb, route, x1, mod_l)
    return x2d.reshape(b, s, d)
```

```python
import functools
import math

import numpy as np
import jax
import jax.numpy as jnp
from jax import lax
from jax.experimental import pallas as pl
from jax.experimental.pallas import tpu as pltpu

D_MODEL = 2048
DEPTH = 2
MLA_HEADS = 8
QK_NOPE_DIM = 128
QK_ROPE_DIM = 64
V_HEAD_DIM = 128
QK_HEAD_DIM = QK_NOPE_DIM + QK_ROPE_DIM
Q_LORA_RANK = 512
KV_LORA_RANK = 256
MLA_WIDTH = MLA_HEADS * V_HEAD_DIM
FNET_GROUPS = 8
FNET_GROUP_DIM = 128
FNET_WIDTH = FNET_GROUPS * FNET_GROUP_DIM
ROPE_THETA = 10000.0
N_EXPERTS = 8
N_MOD = 6
RMS_EPS = 1e-6

LANES = 128
QK_PAD = 256
V_PAD = 256
VMEM_LIMIT_BYTES = 56 * 1024 * 1024
ATT_VMEM_LIMIT_BYTES = 60 * 1024 * 1024

ADA_TN = 1024
ROPE_TN = 2048
MIX_TM = 512
ATT_TQ = 1024
ATT_TC = 1024
ATT_UNROLL = 4
FNET_K1 = 4
FNET_K2 = 4
FFN_TM = 512
FFN_TF = 512
MOE_BLK = 512
MOE_TF = 1024
ROW_TM = 256

BF16 = jnp.bfloat16
F32 = jnp.float32
NT_DIMS = (((1,), (1,)), ((), ()))


def _cparams(sem, vmem=VMEM_LIMIT_BYTES):
    return pltpu.CompilerParams(dimension_semantics=sem, vmem_limit_bytes=vmem)


def _dot(a, b):
    return jnp.dot(a, b, preferred_element_type=F32)


def _rms(x, g):
    ms = jnp.mean(x * x, axis=-1, keepdims=True)
    return (x * lax.rsqrt(ms + RMS_EPS)) * g


def _ada_kernel(c_ref, w_ref, b_ref, o_ref, cond_ref):
    c = c_ref[...]
    cond_ref[...] = c * (1.0 / (1.0 + jnp.exp(-c)))
    d = w_ref.shape[1]
    tn = w_ref.shape[2]
    rows = 64

    def body(i, acc):
        r = pl.multiple_of(i * rows, rows)
        prod = w_ref[0, pl.ds(r, rows), :] * cond_ref[pl.ds(r, rows), :]
        for s in range(rows // 8):
            acc = acc + prod[8 * s:8 * s + 8]
        return acc

    acc = lax.fori_loop(0, d // rows, body, jnp.zeros((8, tn), F32))
    o_ref[0] = jnp.sum(acc, axis=0, keepdims=True) + b_ref[0]


def _ada_mod(c_col, ada_w, ada_b3):
    depth, d, n = ada_w.shape
    tn = min(ADA_TN, n)
    return pl.pallas_call(
        _ada_kernel,
        grid=(depth, n // tn),
        in_specs=[
            pl.BlockSpec((d, 1), lambda l, j: (0, 0)),
            pl.BlockSpec((1, d, tn), lambda l, j: (l, 0, j)),
            pl.BlockSpec((1, 1, tn), lambda l, j: (l, 0, j)),
        ],
        out_specs=pl.BlockSpec((1, 1, tn), lambda l, j: (l, 0, j)),
        out_shape=jax.ShapeDtypeStruct((depth, 1, n), F32),
        scratch_shapes=[pltpu.VMEM((d, 1), F32)],
        compiler_params=_cparams(("arbitrary", "arbitrary")),
        name="ada_mod",
    )(c_col, ada_w, ada_b3)


def _rope_kernel(pos_row_ref, pos_col_ref, invf_col_ref, invf_row_ref, sgn_col_ref, sgn_row_ref,
                 cos_t_ref, sin_t_ref, cos4_ref, sin4_ref):
    ang_t = invf_col_ref[...] * pos_row_ref[...]
    cos_t_ref[...] = jnp.cos(ang_t)
    sin_t_ref[...] = jnp.sin(ang_t) * sgn_col_ref[...]
    ang4 = pos_col_ref[...] * invf_row_ref[...]
    cos4_ref[...] = jnp.cos(ang4)
    sin4_ref[...] = jnp.sin(ang4) * sgn_row_ref[...]


def _rope_tables(pos_f32):
    t = pos_f32.shape[0]
    tn = min(ROPE_TN, t)
    half = QK_ROPE_DIM // 2
    inv_freq = ROPE_THETA ** (-jnp.arange(half, dtype=F32) / half)
    invf_col = jnp.tile(inv_freq, 2).reshape(QK_ROPE_DIM, 1)
    invf_row = jnp.tile(inv_freq, LANES // half).reshape(1, LANES)
    sgn_col = jnp.asarray(np.repeat([-1.0, 1.0], half).reshape(QK_ROPE_DIM, 1), F32)
    sgn_row = jnp.asarray(np.tile(np.repeat([-1.0, 1.0], half), LANES // QK_ROPE_DIM).reshape(1, LANES), F32)
    const = lambda i: (0, 0)
    return pl.pallas_call(
        _rope_kernel,
        grid=(t // tn,),
        in_specs=[
            pl.BlockSpec((1, tn), lambda i: (0, i)),
            pl.BlockSpec((tn, 1), lambda i: (i, 0)),
            pl.BlockSpec((QK_ROPE_DIM, 1), const),
            pl.BlockSpec((1, LANES), const),
            pl.BlockSpec((QK_ROPE_DIM, 1), const),
            pl.BlockSpec((1, LANES), const),
        ],
        out_specs=[
            pl.BlockSpec((QK_ROPE_DIM, tn), lambda i: (0, i)),
            pl.BlockSpec((QK_ROPE_DIM, tn), lambda i: (0, i)),
            pl.BlockSpec((tn, LANES), lambda i: (i, 0)),
            pl.BlockSpec((tn, LANES), lambda i: (i, 0)),
        ],
        out_shape=[
            jax.ShapeDtypeStruct((QK_ROPE_DIM, t), F32),
            jax.ShapeDtypeStruct((QK_ROPE_DIM, t), F32),
            jax.ShapeDtypeStruct((t, LANES), F32),
            jax.ShapeDtypeStruct((t, LANES), F32),
        ],
        compiler_params=_cparams(("arbitrary",)),
        name="rope_tables",
    )(pos_f32.reshape(1, t), pos_f32.reshape(t, 1), invf_col, invf_row, sgn_col, sgn_row)


Q_SCALE = (QK_HEAD_DIM ** -0.5) * math.log2(math.e)


def _mix_in_kernel(x_ref, mod_ref, ng_ref, w_in_ref, w_kr_ref, cqg_ref, ckvg_ref, w_uq_ref, w_uv_ref, w_uk_ref,
                   qgn_ref, qgr_ref, qgs_ref, kgn_ref, kgr_ref, kgs_ref, cos4_ref, sin4_ref, cos_t_ref, sin_t_ref,
                   q_ref, kt_ref, v_ref, u_ref):
    d = x_ref.shape[1]
    x = x_ref[...]
    shift = mod_ref[:, 0:d]
    scale = mod_ref[:, d:2 * d]
    h = _rms(x, ng_ref[...]) * (1.0 + scale) + shift
    hb = h.astype(BF16)
    z = _dot(hb, w_in_ref[...])
    c_q = z[:, 0:Q_LORA_RANK]
    c_kv = z[:, Q_LORA_RANK:Q_LORA_RANK + KV_LORA_RANK]
    u_ref[...] = z[:, Q_LORA_RANK + KV_LORA_RANK:]
    cqn = _rms(c_q, cqg_ref[...]).astype(BF16)
    ckvn = _rms(c_kv, ckvg_ref[...]).astype(BF16)

    kn_t = lax.dot_general(w_uk_ref[...], ckvn, NT_DIMS, preferred_element_type=F32)
    kr_t = lax.dot_general(w_kr_ref[...], hb, NT_DIMS, preferred_element_type=F32)
    kr = kr_t[0:QK_ROPE_DIM]
    kr_rot = (kr * kgr_ref[...]) * cos_t_ref[...] + (kr_t[QK_ROPE_DIM:] * kgs_ref[...]) * sin_t_ref[...]
    ss_r = jnp.sum(kr * kr, axis=0, keepdims=True)
    tm = x.shape[0]
    pad_rows = jnp.zeros((QK_PAD - QK_HEAD_DIM, tm), BF16)
    for hd in range(MLA_HEADS):
        kn = kn_t[hd * QK_NOPE_DIM:(hd + 1) * QK_NOPE_DIM]
        ss = jnp.sum(kn * kn, axis=0, keepdims=True) + ss_r
        r = lax.rsqrt(ss * (1.0 / QK_HEAD_DIM) + RMS_EPS)
        kt_ref[hd, 0:QK_NOPE_DIM, :] = ((kn * r) * kgn_ref[...]).astype(BF16)
        kt_ref[hd, QK_NOPE_DIM:QK_HEAD_DIM, :] = (kr_rot * r).astype(BF16)
        kt_ref[hd, QK_HEAD_DIM:, :] = pad_rows

    vz = _dot(ckvn, w_uv_ref[...])
    ones = jnp.ones((tm, V_PAD - V_HEAD_DIM), BF16)
    for hd in range(MLA_HEADS):
        v_ref[hd, :, 0:V_HEAD_DIM] = vz[:, hd * V_HEAD_DIM:(hd + 1) * V_HEAD_DIM].astype(BF16)
        v_ref[hd, :, V_HEAD_DIM:] = ones

    qz = _dot(cqn, w_uq_ref[...])
    n_nope = MLA_HEADS * QK_NOPE_DIM
    n_rope = MLA_HEADS * QK_ROPE_DIM
    lane = lax.broadcasted_iota(jnp.int32, (x.shape[0], LANES), 1)
    for pair in range(MLA_HEADS // 2):
        pr = qz[:, n_nope + pair * LANES:n_nope + (pair + 1) * LANES]
        ps = qz[:, n_nope + n_rope + pair * LANES:n_nope + n_rope + (pair + 1) * LANES]
        rot = (pr * qgr_ref[...]) * cos4_ref[...] + (ps * qgs_ref[...]) * sin4_ref[...]
        pr2 = pr * pr
        for hd in (2 * pair, 2 * pair + 1):
            nope = qz[:, hd * QK_NOPE_DIM:(hd + 1) * QK_NOPE_DIM]
            own = (lane < QK_ROPE_DIM) if hd % 2 == 0 else (lane >= QK_ROPE_DIM)
            ss = jnp.sum(nope * nope + jnp.where(own, pr2, 0.0), axis=-1, keepdims=True)
            r = lax.rsqrt(ss * (1.0 / QK_HEAD_DIM) + RMS_EPS) * Q_SCALE
            q_ref[hd, :, 0:QK_NOPE_DIM] = ((nope * r) * qgn_ref[...]).astype(BF16)
            rope = rot * r
            if hd % 2 == 1:
                rope = pltpu.roll(rope, QK_ROPE_DIM, 1)
            q_ref[hd, :, QK_NOPE_DIM:] = jnp.where(lane < QK_ROPE_DIM, rope, 0.0).astype(BF16)


def _mix_in(x2d, mod_l, ng, wp, cos4, sin4, cos_t, sin_t):
    t, d = x2d.shape
    tm = min(MIX_TM, t)
    hh = MLA_HEADS
    const2 = lambda i: (0, 0)
    full = lambda a: pl.BlockSpec(a.shape, const2)
    ins = [x2d, mod_l, ng, wp["w_in"], wp["w_kr"], wp["cq_g"], wp["ckv_g"], wp["w_uq"], wp["w_uv"], wp["w_uk"],
           wp["qg_n"], wp["qg_r"], wp["qg_s"], wp["kg_n"], wp["kg_r"], wp["kg_s"]]
    in_specs = [pl.BlockSpec((tm, d), lambda i: (i, 0))] + [full(a) for a in ins[1:]]
    ins += [cos4, sin4, cos_t, sin_t]
    in_specs += [
        pl.BlockSpec((tm, LANES), lambda i: (i, 0)),
        pl.BlockSpec((tm, LANES), lambda i: (i, 0)),
        pl.BlockSpec((QK_ROPE_DIM, tm), lambda i: (0, i)),
        pl.BlockSpec((QK_ROPE_DIM, tm), lambda i: (0, i)),
    ]
    return pl.pallas_call(
        _mix_in_kernel,
        grid=(t // tm,),
        in_specs=in_specs,
        out_specs=[
            pl.BlockSpec((hh, tm, QK_PAD), lambda i: (0, i, 0)),
            pl.BlockSpec((hh, QK_PAD, tm), lambda i: (0, 0, i)),
            pl.BlockSpec((hh, tm, V_PAD), lambda i: (0, i, 0)),
            pl.BlockSpec((tm, FNET_WIDTH), lambda i: (i, 0)),
        ],
        out_shape=[
            jax.ShapeDtypeStruct((hh, t, QK_PAD), BF16),
            jax.ShapeDtypeStruct((hh, QK_PAD, t), BF16),
            jax.ShapeDtypeStruct((hh, t, V_PAD), BF16),
            jax.ShapeDtypeStruct((t, FNET_WIDTH), F32),
        ],
        compiler_params=_cparams(("arbitrary",)),
        name="mix_in",
    )(*ins)


ATT_FAST_BOUND = 48.0


def _score_bound(q_gain, k_gain):
    return (Q_SCALE * QK_HEAD_DIM) * jnp.max(jnp.abs(q_gain)) * jnp.max(jnp.abs(k_gain))


def _attn_kernel(fast_ref, q_ref, kt_ref, v_ref, *rest, tc, unroll, n_side):
    side_in = rest[:n_side]
    o_ref = rest[n_side]
    side_out = rest[n_side + 1:2 * n_side + 1]
    acc_ref, m_ref = rest[2 * n_side + 1:]
    n_chunks = kt_ref.shape[2] // tc
    acc_ref[...] = jnp.zeros(acc_ref.shape, F32)

    for wi_ref, wo_ref in zip(side_in, side_out):
        wo_ref[...] = wi_ref[...].astype(BF16)

    def chunk(c):
        off = pl.multiple_of(c * tc, tc)
        return kt_ref[0, :, pl.ds(off, tc)], v_ref[0, pl.ds(off, tc), :]

    @pl.when(fast_ref[0] == 1)
    def _():
        def body(c2, carry):
            for u in range(unroll):
                k_c, v_c = chunk(unroll * c2 + u)
                p = jnp.exp2(_dot(q_ref[0], k_c)).astype(BF16)
                acc_ref[...] += _dot(p, v_c)
            return carry

        lax.fori_loop(0, n_chunks // unroll, body, 0)

    @pl.when(fast_ref[0] != 1)
    def _():
        m_ref[...] = jnp.full(m_ref.shape, -jnp.inf, F32)

        def body(c, carry):
            k_c, v_c = chunk(c)
            s = _dot(q_ref[0], k_c)
            m_prev = m_ref[...]
            m_new = jnp.maximum(m_prev, jnp.max(s, axis=-1, keepdims=True))
            p = jnp.exp2(s - m_new).astype(BF16)
            acc_ref[...] = jnp.exp2(m_prev - m_new) * acc_ref[...] + _dot(p, v_c)
            m_ref[...] = m_new
            return carry

        lax.fori_loop(0, n_chunks, body, 0)

    o_ref[...] = (acc_ref[:, 0:V_HEAD_DIM] * (1.0 / acc_ref[:, V_HEAD_DIM:])).astype(o_ref.dtype)


def _attention(fast, q, kt, v, side=()):
    hh, t, _ = q.shape
    tq = min(ATT_TQ, t)
    tc = min(ATT_TC, t)
    unroll = min(ATT_UNROLL, t // tc)
    once = pl.Buffered(1)
    side_specs = [pl.BlockSpec(blk, (lambda h, i, f, im=im: im(h, i))) for _, blk, im in side]
    outs = pl.pallas_call(
        functools.partial(_attn_kernel, tc=tc, unroll=unroll, n_side=len(side)),
        grid_spec=pltpu.PrefetchScalarGridSpec(
            num_scalar_prefetch=1,
            grid=(hh, t // tq),
            in_specs=[
                pl.BlockSpec((1, tq, QK_PAD), lambda h, i, f: (h, i, 0)),
                pl.BlockSpec((1, QK_PAD, t), lambda h, i, f: (h, 0, 0), pipeline_mode=once),
                pl.BlockSpec((1, t, V_PAD), lambda h, i, f: (h, 0, 0), pipeline_mode=once),
            ] + side_specs,
            out_specs=[pl.BlockSpec((tq, V_HEAD_DIM), lambda h, i, f: (i, h))] + side_specs,
            scratch_shapes=[pltpu.VMEM((tq, V_PAD), F32), pltpu.VMEM((tq, 1), F32)],
        ),
        out_shape=[jax.ShapeDtypeStruct((t, hh * V_HEAD_DIM), BF16)]
        + [jax.ShapeDtypeStruct(w.shape, BF16) for w, _, _ in side],
        compiler_params=_cparams(("arbitrary", "arbitrary"), vmem=ATT_VMEM_LIMIT_BYTES),
        name="attention",
    )(fast, q, kt, v, *[w for w, _, _ in side])
    return outs


def _fnet1_kernel(u_ref, f1_ref, twc_ref, tws_ref, zr_ref, zi_ref):
    k1, n2, width = zr_ref.shape
    y = _dot(f1_ref[...], u_ref[...].astype(BF16))
    a = y[0:n2]
    b = y[n2:]
    for i in range(k1):
        ai = a[:, i * width:(i + 1) * width]
        bi = b[:, i * width:(i + 1) * width]
        c = twc_ref[i]
        s = tws_ref[i]
        zr_ref[i] = (ai * c - bi * s).astype(zr_ref.dtype)
        zi_ref[i] = (-(bi * c) - ai * s).astype(zi_ref.dtype)


def _fnet2_kernel(zr_ref, zi_ref, f2_ref, cs_ref, wf_ref, o_ref):
    n1 = zr_ref.shape[0]
    k2 = zr_ref.shape[1] // FNET_WIDTH
    gd = FNET_GROUP_DIM
    zz = jnp.concatenate([zr_ref[...], zi_ref[...]], axis=0)
    x = _dot(f2_ref[...], zz)
    xr = x[0:n1]
    xi = x[n1:]
    for g in range(FNET_GROUPS):
        rows = []
        for j in range(k2):
            c0 = j * FNET_WIDTH + g * gd
            rows.append(jnp.concatenate([xr[:, c0:c0 + gd], xi[:, c0:c0 + gd]], axis=1))
        lhs = jnp.concatenate(rows, axis=0).astype(BF16)
        fg = _dot(lhs, cs_ref[...]).astype(BF16)
        og = _dot(fg, wf_ref[g])
        for j in range(k2):
            c0 = j * FNET_WIDTH + g * gd
            o_ref[:, c0:c0 + gd] = og[j * n1:(j + 1) * n1].astype(o_ref.dtype)


def _dft_consts(n1, n2):
    n = n1 * n2
    a2 = 2.0 * np.pi * np.outer(np.arange(n2), np.arange(n2)) / n2
    f1 = np.concatenate([np.cos(a2), np.sin(a2)], axis=0) / n2
    atw = 2.0 * np.pi * np.outer(np.arange(n1), np.arange(n2)) / n
    a1 = 2.0 * np.pi * np.outer(np.arange(n1), np.arange(n1)) / n1
    c1, s1 = np.cos(a1), np.sin(a1)
    f2 = np.block([[c1, s1], [-s1, c1]])
    ac = 2.0 * np.pi * np.outer(np.arange(FNET_GROUP_DIM), np.arange(FNET_GROUP_DIM)) / FNET_GROUP_DIM
    norm = n2 / math.sqrt(n * FNET_GROUP_DIM)
    cs = np.concatenate([np.cos(ac), np.sin(ac)], axis=0) * norm
    return (jnp.asarray(f1, BF16), jnp.asarray(np.cos(atw)[:, :, None], F32), jnp.asarray(np.sin(atw)[:, :, None], F32),
            jnp.asarray(f2, BF16), jnp.asarray(cs, BF16))


def _fnet(u, wf_b, n1, n2):
    t, width = u.shape
    assert t == n1 * n2
    f1, twc, tws, f2, cs = _dft_consts(n1, n2)
    k1 = min(FNET_K1, n1)
    k2 = min(FNET_K2, n2)
    const2 = lambda i: (0, 0)
    zr, zi = pl.pallas_call(
        _fnet1_kernel,
        grid=(n1 // k1,),
        in_specs=[
            pl.BlockSpec((n2, k1 * width), lambda i: (0, i)),
            pl.BlockSpec(f1.shape, const2),
            pl.BlockSpec((k1, n2, 1), lambda i: (i, 0, 0)),
            pl.BlockSpec((k1, n2, 1), lambda i: (i, 0, 0)),
        ],
        out_specs=[pl.BlockSpec((k1, n2, width), lambda i: (i, 0, 0))] * 2,
        out_shape=[jax.ShapeDtypeStruct((n1, n2, width), BF16)] * 2,
        compiler_params=_cparams(("arbitrary",)),
        name="fnet_stage1",
    )(u.reshape(n2, n1 * width), f1, twc, tws)
    out = pl.pallas_call(
        _fnet2_kernel,
        grid=(n2 // k2,),
        in_specs=[
            pl.BlockSpec((n1, k2 * width), lambda i: (0, i)),
            pl.BlockSpec((n1, k2 * width), lambda i: (0, i)),
            pl.BlockSpec(f2.shape, const2),
            pl.BlockSpec(cs.shape, const2),
            pl.BlockSpec(wf_b.shape, lambda i: (0, 0, 0)),
        ],
        out_specs=pl.BlockSpec((n1, k2 * width), lambda i: (0, i)),
        out_shape=jax.ShapeDtypeStruct((n1, n2 * width), BF16),
        compiler_params=_cparams(("arbitrary",)),
        name="fnet_stage2",
    )(zr.reshape(n1, n2 * width), zi.reshape(n1, n2 * width), f2, cs, wf_b)
    return out.reshape(t, width)


def _mix_out_kernel(a_ref, f_ref, x_ref, mod_ref, og_ref, w_out_ref, fg_ref, *rest, moe):
    d = x_ref.shape[1]
    half = a_ref.shape[1]
    an = _rms(a_ref[...].astype(F32), og_ref[:, 0:half])
    fn = _rms(f_ref[...].astype(F32), og_ref[:, half:])
    cat = jnp.concatenate([an, fn], axis=1).astype(BF16)
    o = _dot(cat, w_out_ref[...])
    gate1 = mod_ref[:, 2 * d:3 * d]
    shift2 = mod_ref[:, 3 * d:4 * d]
    scale2 = mod_ref[:, 4 * d:5 * d]
    x1 = x_ref[...] + gate1 * o
    h2 = _rms(x1, fg_ref[...]) * (1.0 + scale2) + shift2
    if not moe:
        x1_ref, h2_ref = rest
        x1_ref[...] = x1
        h2_ref[...] = h2.astype(h2_ref.dtype)
        return
    wr_ref, tri_ref, x1_ref, h2_ref, route_ref, cnt_ref, carry_ref = rest
    x1_ref[...] = x1
    h2_ref[...] = h2

    @pl.when(pl.program_id(0) == 0)
    def _():
        carry_ref[...] = jnp.zeros(carry_ref.shape, F32)

    tm = x1.shape[0]
    h_hi = h2.astype(BF16)
    h_lo = (h2 - h_hi.astype(F32)).astype(BF16)
    hw = _dot(h_hi, wr_ref[...])
    logits = (hw[:, 0:LANES] + hw[:, LANES:]) + _dot(h_lo, wr_ref[:, 0:LANES])
    lane = lax.broadcasted_iota(jnp.int32, (tm, LANES), 1)
    neg = jnp.float32(-jnp.inf)
    logits = jnp.where(lane < N_EXPERTS, logits, neg)
    m1 = jnp.max(logits, axis=-1, keepdims=True)
    i1 = jnp.min(jnp.where(logits == m1, lane, LANES), axis=-1, keepdims=True)
    rest_l = jnp.where(lane == i1, neg, logits)
    m2 = jnp.max(rest_l, axis=-1, keepdims=True)
    i2 = jnp.min(jnp.where(rest_l == m2, lane, LANES), axis=-1, keepdims=True)
    e21 = jnp.exp(m2 - m1)
    w1 = 1.0 / (1.0 + e21)
    w2 = e21 / (1.0 + e21)
    oh1 = jnp.where(lane == i1, 1.0, 0.0)
    oh2 = jnp.where(lane == i2, 1.0, 0.0)
    pre = _dot(tri_ref[...], jnp.concatenate([oh1, oh2], axis=1).astype(BF16))
    cnt1 = jnp.sum(oh1, axis=0, keepdims=True)
    cnt2 = jnp.sum(oh2, axis=0, keepdims=True)
    carry = carry_ref[0:1, :]
    rank1 = jnp.sum(oh1 * (carry + pre[:, 0:LANES]), axis=-1, keepdims=True)
    rank2 = jnp.sum(oh2 * (carry + cnt1 + pre[:, LANES:]), axis=-1, keepdims=True)
    new_carry = carry + cnt1 + cnt2
    carry_ref[...] = jnp.broadcast_to(new_carry, carry_ref.shape)
    cnt_ref[...] = jnp.broadcast_to(new_carry, cnt_ref.shape)
    vals = (i1.astype(F32), i2.astype(F32), w1, w2, rank1, rank2)
    route = jnp.zeros((tm, LANES), F32)
    for k, val in enumerate(vals):
        route = jnp.where(lane == k, val, route)
    route_ref[...] = route


def _mix_out(a, f, x2d, mod_l, og, w_out_b, fg, router=None):
    t, d = x2d.shape
    tm = min(MIX_TM, t)
    half = a.shape[1]
    moe = router is not None
    const2 = lambda i: (0, 0)
    row = lambda w: pl.BlockSpec((tm, w), lambda i: (i, 0))
    ins = [a, f, x2d, mod_l, og, w_out_b, fg]
    in_specs = [row(half), row(half), row(d), pl.BlockSpec(mod_l.shape, const2), pl.BlockSpec(og.shape, const2),
                pl.BlockSpec(w_out_b.shape, const2), pl.BlockSpec(fg.shape, const2)]
    out_specs = [row(d), row(d)]
    out_shape = [jax.ShapeDtypeStruct((t, d), F32), jax.ShapeDtypeStruct((t, d), F32 if moe else BF16)]
    scratch = []
    if moe:
        tri = jnp.asarray(np.tril(np.ones((tm, tm), np.float32), -1), BF16)
        ins += [router, tri]
        in_specs += [pl.BlockSpec(router.shape, const2), pl.BlockSpec(tri.shape, const2)]
        out_specs += [row(LANES), pl.BlockSpec((8, LANES), const2)]
        out_shape += [jax.ShapeDtypeStruct((t, LANES), F32), jax.ShapeDtypeStruct((8, LANES), F32)]
        scratch = [pltpu.VMEM((8, LANES), F32)]
    return pl.pallas_call(
        functools.partial(_mix_out_kernel, moe=moe),
        grid=(t // tm,),
        in_specs=in_specs,
        out_specs=out_specs,
        out_shape=out_shape,
        scratch_shapes=scratch,
        compiler_params=_cparams(("arbitrary",)),
        name="mix_out_moe" if moe else "mix_out",
    )(*ins)


def _silu(g):
    return g * (1.0 / (1.0 + jnp.exp(-g)))


def _ffn_kernel(h_ref, x1_ref, mod_ref, wg_ref, wu_ref, wd_ref, o_ref):
    j = pl.program_id(1)
    d = x1_ref.shape[1]
    @pl.when(j == 0)
    def _():
        o_ref[...] = jnp.zeros(o_ref.shape, F32)

    h = h_ref[...]
    act = (_silu(_dot(h, wg_ref[...])) * _dot(h, wu_ref[...])).astype(BF16)
    o_ref[...] += _dot(act, wd_ref[...])

    @pl.when(j == pl.num_programs(1) - 1)
    def _():
        o_ref[...] = x1_ref[...] + mod_ref[:, 5 * d:6 * d] * o_ref[...]


def _ffn_dense(h2, x1, mod_l, wg, wu, wd):
    t, d = x1.shape
    ff = wg.shape[1]
    tm = min(FFN_TM, t)
    tf = min(FFN_TF, ff)
    return pl.pallas_call(
        _ffn_kernel,
        grid=(t // tm, ff // tf),
        in_specs=[
            pl.BlockSpec((tm, d), lambda i, j: (i, 0)),
            pl.BlockSpec((tm, d), lambda i, j: (i, 0)),
            pl.BlockSpec(mod_l.shape, lambda i, j: (0, 0)),
            pl.BlockSpec((d, tf), lambda i, j: (0, j)),
            pl.BlockSpec((d, tf), lambda i, j: (0, j)),
            pl.BlockSpec((tf, d), lambda i, j: (j, 0)),
        ],
        out_specs=pl.BlockSpec((tm, d), lambda i, j: (i, 0)),
        out_shape=jax.ShapeDtypeStruct((t, d), F32),
        compiler_params=_cparams(("arbitrary", "arbitrary")),
        name="ffn_dense",
    )(h2, x1, mod_l, wg, wu, wd)


def _dispatch_kernel(d1_ref, d2_ref, h_ref, xs_in_ref, xs_ref, sem):
    del xs_in_ref
    tm = h_ref.shape[0]
    base = pl.program_id(0) * tm

    def copy(r, dst):
        return pltpu.make_async_copy(h_ref.at[pl.ds(r, 1), :], xs_ref.at[pl.ds(dst, 1), :], sem)

    def start(r, c):
        copy(r, d1_ref[base + r]).start()
        copy(r, d2_ref[base + r]).start()
        return c

    def wait(r, c):
        copy(r, d1_ref[base + r]).wait()
        copy(r, d2_ref[base + r]).wait()
        return c

    lax.fori_loop(0, tm, start, 0)
    lax.fori_loop(0, tm, wait, 0)


def _dispatch(dest1, dest2, h2, n_rows):
    t, d = h2.shape
    tm = min(ROW_TM, t)
    xs0 = jnp.zeros((n_rows, d), h2.dtype)
    return pl.pallas_call(
        _dispatch_kernel,
        grid_spec=pltpu.PrefetchScalarGridSpec(
            num_scalar_prefetch=2,
            grid=(t // tm,),
            in_specs=[pl.BlockSpec((tm, d), lambda i, d1, d2: (i, 0)), pl.BlockSpec(memory_space=pl.ANY)],
            out_specs=pl.BlockSpec(memory_space=pl.ANY),
            scratch_shapes=[pltpu.SemaphoreType.DMA(())],
        ),
        out_shape=jax.ShapeDtypeStruct((n_rows, d), h2.dtype),
        input_output_aliases={3: 0},
        compiler_params=_cparams(("arbitrary",)),
        name="moe_dispatch",
    )(dest1, dest2, h2, xs0)


def _expert_kernel(be_ref, nu_ref, x_ref, wg_ref, wu_ref, wd_ref, o_ref, xb_ref):
    b = pl.program_id(0)
    j = pl.program_id(1)

    @pl.when(j == 0)
    def _():
        o_ref[...] = jnp.zeros(o_ref.shape, o_ref.dtype)
        xb_ref[...] = x_ref[...].astype(BF16)

    @pl.when(b < nu_ref[0])
    def _():
        xb = xb_ref[...]
        act = (_silu(_dot(xb, wg_ref[0])) * _dot(xb, wu_ref[0])).astype(BF16)
        o_ref[...] += _dot(act, wd_ref[0])


def _experts(block_e, n_used, xs, wg, wu, wd):
    n_rows, d = xs.shape
    ne, _, ff = wg.shape
    blk = min(MOE_BLK, n_rows)
    tf = min(MOE_TF, ff)
    nb = n_rows // blk
    nj = ff // tf

    def last_live(b, nu):
        return jnp.maximum(jnp.minimum(b, nu[0] - 1), 0)

    def row_map(b, j, be, nu):
        return (last_live(b, nu), 0)

    def col_map(b, j, be, nu):
        return (be[last_live(b, nu)], 0, jnp.where(b < nu[0], j, nj - 1))

    def down_map(b, j, be, nu):
        return (be[last_live(b, nu)], jnp.where(b < nu[0], j, nj - 1), 0)

    return pl.pallas_call(
        _expert_kernel,
        grid_spec=pltpu.PrefetchScalarGridSpec(
            num_scalar_prefetch=2,
            grid=(nb, nj),
            in_specs=[
                pl.BlockSpec((blk, d), row_map),
                pl.BlockSpec((1, d, tf), col_map),
                pl.BlockSpec((1, d, tf), col_map),
                pl.BlockSpec((1, tf, d), down_map),
            ],
            out_specs=pl.BlockSpec((blk, d), lambda b, j, be, nu: (b, 0)),
            scratch_shapes=[pltpu.VMEM((blk, d), BF16)],
        ),
        out_shape=jax.ShapeDtypeStruct((n_rows, d), F32),
        compiler_params=_cparams(("arbitrary", "arbitrary")),
        name="moe_experts",
    )(block_e, n_used, xs, wg, wu, wd)


def _combine_kernel(d1_ref, d2_ref, yb_ref, route_ref, x1_ref, mod_ref, o_ref, buf_ref, sem):
    tm, d = x1_ref.shape
    base = pl.program_id(0) * tm

    def copy(slot, r, src):
        return pltpu.make_async_copy(yb_ref.at[pl.ds(src, 1), :], buf_ref.at[slot, pl.ds(r, 1), :], sem)

    def start(r, c):
        copy(0, r, d1_ref[base + r]).start()
        copy(1, r, d2_ref[base + r]).start()
        return c

    def wait(r, c):
        copy(0, r, d1_ref[base + r]).wait()
        copy(1, r, d2_ref[base + r]).wait()
        return c

    lax.fori_loop(0, tm, start, 0)
    lax.fori_loop(0, tm, wait, 0)
    w1 = route_ref[:, 2:3]
    w2 = route_ref[:, 3:4]
    y = buf_ref[0] * w1 + buf_ref[1] * w2
    o_ref[...] = x1_ref[...] + mod_ref[:, 5 * d:6 * d] * y


def _combine(dest1, dest2, yb, route, x1, mod_l):
    t, d = x1.shape
    tm = min(ROW_TM, t)
    row = lambda w: pl.BlockSpec((tm, w), lambda i, d1, d2: (i, 0))
    return pl.pallas_call(
        _combine_kernel,
        grid_spec=pltpu.PrefetchScalarGridSpec(
            num_scalar_prefetch=2,
            grid=(t // tm,),
            in_specs=[pl.BlockSpec(memory_space=pl.ANY), row(LANES), row(d),
                      pl.BlockSpec(mod_l.shape, lambda i, d1, d2: (0, 0))],
            out_specs=row(d),
            scratch_shapes=[pltpu.VMEM((2, tm, d), F32), pltpu.SemaphoreType.DMA(())],
        ),
        out_shape=jax.ShapeDtypeStruct((t, d), F32),
        compiler_params=_cparams(("arbitrary",)),
        name="moe_combine",
    )(dest1, dest2, yb, route, x1, mod_l)


def _moe_plan(route, cnt, t, blk):
    e1 = route[:, 0].astype(jnp.int32)
    e2 = route[:, 1].astype(jnp.int32)
    r1 = route[:, 4].astype(jnp.int32)
    r2 = route[:, 5].astype(jnp.int32)
    counts = cnt[0, :N_EXPERTS].astype(jnp.int32)
    padded = (counts + blk - 1) // blk * blk
    pend = jnp.cumsum(padded)
    pstart = pend - padded
    eids = jnp.arange(N_EXPERTS, dtype=jnp.int32)
    dest1 = jnp.sum(jnp.where(e1[:, None] == eids[None, :], pstart[None, :], 0), axis=1) + r1
    dest2 = jnp.sum(jnp.where(e2[:, None] == eids[None, :], pstart[None, :], 0), axis=1) + r2
    n_blocks = -(-(2 * t) // blk) + N_EXPERTS
    bstart = jnp.arange(n_blocks, dtype=jnp.int32) * blk
    block_e = jnp.minimum(jnp.sum((pend[None, :] <= bstart[:, None]).astype(jnp.int32), axis=1), N_EXPERTS - 1)
    n_used = (pend[-1] // blk).astype(jnp.int32).reshape(1)
    return dest1.astype(jnp.int32), dest2.astype(jnp.int32), block_e.astype(jnp.int32), n_used, n_blocks * blk


def _prep_mixer_weights(w_in, cq_norm, ckv_norm, w_uq, w_ukv, q_norm, k_norm):
    s1 = Q_LORA_RANK
    s2 = s1 + KV_LORA_RANK
    s3 = s2 + QK_ROPE_DIM
    half = QK_ROPE_DIM // 2
    swap = np.concatenate([np.arange(half, QK_ROPE_DIM), np.arange(half)])
    w_kr = w_in[:, s2:s3]
    uq = w_uq.reshape(Q_LORA_RANK, MLA_HEADS, QK_HEAD_DIM)
    uq_rope = uq[:, :, QK_NOPE_DIM:]
    ukv = w_ukv.reshape(KV_LORA_RANK, MLA_HEADS, QK_NOPE_DIM + V_HEAD_DIM)
    qg_r = q_norm[QK_NOPE_DIM:]
    kg_r = k_norm[QK_NOPE_DIM:]
    reps = LANES // QK_ROPE_DIM
    return {
        "w_in": jnp.concatenate([w_in[:, :s2], w_in[:, s3:]], axis=1).astype(BF16),
        "w_kr": jnp.concatenate([w_kr, w_kr[:, swap]], axis=1).T.astype(BF16),
        "cq_g": cq_norm.reshape(1, -1),
        "ckv_g": ckv_norm.reshape(1, -1),
        "w_uq": jnp.concatenate([uq[:, :, :QK_NOPE_DIM].reshape(Q_LORA_RANK, -1),
                                 uq_rope.reshape(Q_LORA_RANK, -1),
                                 uq_rope[:, :, swap].reshape(Q_LORA_RANK, -1)], axis=1).astype(BF16),
        "w_uv": ukv[:, :, QK_NOPE_DIM:].reshape(KV_LORA_RANK, -1).astype(BF16),
        "w_uk": ukv[:, :, :QK_NOPE_DIM].reshape(KV_LORA_RANK, -1).T.astype(BF16),
        "qg_n": q_norm[:QK_NOPE_DIM].reshape(1, -1),
        "qg_r": jnp.tile(qg_r, reps).reshape(1, -1),
        "qg_s": jnp.tile(qg_r[swap], reps).reshape(1, -1),
        "kg_n": k_norm[:QK_NOPE_DIM].reshape(-1, 1),
        "kg_r": kg_r.reshape(-1, 1),
        "kg_s": kg_r[swap].reshape(-1, 1),
    }


BF16_ROWS = 16


def _row_blocked(w, n_qt):
    steps = MLA_HEADS * n_qt
    tiles = w.shape[0] // BF16_ROWS
    n_blk = max(k for k in range(1, steps + 1) if tiles % k == 0)
    return w, (w.shape[0] // n_blk, w.shape[1]), lambda h, i: (jnp.minimum(h * n_qt + i, n_blk - 1), 0)


def _expert_blocked(w, n_qt):
    assert w.shape[0] == MLA_HEADS and w.shape[1] % (n_qt * BF16_ROWS) == 0
    return w, (1, w.shape[1] // n_qt, w.shape[2]), lambda h, i: (h, i, 0)


def _fnet_factors(t):
    n2 = 1 << (int(math.log2(t)) // 2)
    return t // n2, n2


def kernel(x, c, positions, ada_w, ada_b, mix_norm, w_in, cq_norm, ckv_norm, w_uq, w_ukv, q_norm, k_norm, w_fnet,
           out_norm, w_out, ffn_norm, dense_w_gate, dense_w_up, dense_w_down, router_w, moe_w_gate, moe_w_up,
           moe_w_down):
    b, s, d = x.shape
    assert b == 1
    t = b * s
    depth = ada_w.shape[0]
    assert depth == 2 and dense_w_gate.shape[0] == 1 and moe_w_gate.shape[0] == 1
    x2d = x.reshape(t, d)
    mod = _ada_mod(c.reshape(d, 1), ada_w, ada_b.reshape(depth, 1, -1))
    cos_t, sin_t, cos4, sin4 = _rope_tables(positions.reshape(t).astype(F32))
    n1, n2 = _fnet_factors(t)
    n_qt = t // min(ATT_TQ, t)
    side_jobs = [
        [_row_blocked(dense_w_gate[0], n_qt), _row_blocked(dense_w_up[0], n_qt), _row_blocked(dense_w_down[0], n_qt),
         _expert_blocked(moe_w_gate[0], n_qt)],
        [_expert_blocked(moe_w_up[0], n_qt), _expert_blocked(moe_w_down[0], n_qt)],
    ]
    cast = []
    for l in range(depth):
        mod_l = mod[l]
        wp = _prep_mixer_weights(w_in[l], cq_norm[l], ckv_norm[l], w_uq[l], w_ukv[l], q_norm[l], k_norm[l])
        fast = (_score_bound(q_norm[l], k_norm[l]) <= ATT_FAST_BOUND).astype(jnp.int32).reshape(1)
        q, kt, v, u = _mix_in(x2d, mod_l, mix_norm[l].reshape(1, d), wp, cos4, sin4, cos_t, sin_t)
        a, *new_cast = _attention(fast, q, kt, v, side_jobs[l])
        cast += new_cast
        f = _fnet(u, w_fnet[l].astype(BF16), n1, n2)
        og = out_norm[l].reshape(1, d)
        fg = ffn_norm[l].reshape(1, d)
        w_out_b = w_out[l].astype(BF16)
        j = l // 2
        if l % 2 == 0:
            x1, h2 = _mix_out(a, f, x2d, mod_l, og, w_out_b, fg)
            x2d = _ffn_dense(h2, x1, mod_l, cast[0], cast[1], cast[2])
        else:
            wr = jnp.pad(router_w[j], ((0, 0), (0, LANES - N_EXPERTS)))
            wr_hi = wr.astype(BF16)
            router = jnp.concatenate([wr_hi, (wr - wr_hi.astype(F32)).astype(BF16)], axis=1)
            x1, h2, route, cnt = _mix_out(a, f, x2d, mod_l, og, w_out_b, fg, router=router)
            dest1, dest2, block_e, n_used, n_rows = _moe_plan(route, cnt, t, MOE_BLK)
            xs = _dispatch(dest1, dest2, h2, n_rows)
            yb = _experts(block_e, n_used, xs, cast[3], cast[4], cast[5])
            x2d = _combine(dest1, dest2, yb, route, x1, mod_l)
    return x2d.reshape(b, s, d)
```

```python
import functools
import math

import numpy as np
import jax
import jax.numpy as jnp
from jax import lax
from jax.experimental import pallas as pl
from jax.experimental.pallas import tpu as pltpu

D_MODEL = 2048
DEPTH = 2
MLA_HEADS = 8
QK_NOPE_DIM = 128
QK_ROPE_DIM = 64
V_HEAD_DIM = 128
QK_HEAD_DIM = QK_NOPE_DIM + QK_ROPE_DIM
Q_LORA_RANK = 512
KV_LORA_RANK = 256
MLA_WIDTH = MLA_HEADS * V_HEAD_DIM
FNET_GROUPS = 8
FNET_GROUP_DIM = 128
FNET_WIDTH = FNET_GROUPS * FNET_GROUP_DIM
ROPE_THETA = 10000.0
N_EXPERTS = 8
N_MOD = 6
RMS_EPS = 1e-6

LANES = 128
QK_PAD = 256
V_PAD = 256
VMEM_LIMIT_BYTES = 56 * 1024 * 1024
ATT_VMEM_LIMIT_BYTES = 60 * 1024 * 1024

ADA_TN = 1024
ROPE_TN = 2048
MIX_TM = 512
ATT_TQ = 1024
ATT_TC = 1024
ATT_UNROLL = 8
FNET_K1 = 4
FNET_K2 = 4
FFN_TM = 512
FFN_TF = 512
MOE_BLK = 512
MOE_TF = 1024
ROW_TM = 256
ROW_DMA_UNROLL = 8

BF16 = jnp.bfloat16
F32 = jnp.float32
NT_DIMS = (((1,), (1,)), ((), ()))


def _cparams(sem, vmem=VMEM_LIMIT_BYTES):
    return pltpu.CompilerParams(dimension_semantics=sem, vmem_limit_bytes=vmem)


def _dot(a, b):
    return jnp.dot(a, b, preferred_element_type=F32)


def _rms(x, g):
    ms = jnp.mean(x * x, axis=-1, keepdims=True)
    return (x * lax.rsqrt(ms + RMS_EPS)) * g


def _ada_kernel(c_ref, w_ref, b_ref, o_ref, cond_ref):
    c = c_ref[...]
    cond_ref[...] = c * (1.0 / (1.0 + jnp.exp(-c)))
    d = w_ref.shape[1]
    tn = w_ref.shape[2]
    rows = 64

    def body(i, acc):
        r = pl.multiple_of(i * rows, rows)
        prod = w_ref[0, pl.ds(r, rows), :] * cond_ref[pl.ds(r, rows), :]
        for s in range(rows // 8):
            acc = acc + prod[8 * s:8 * s + 8]
        return acc

    acc = lax.fori_loop(0, d // rows, body, jnp.zeros((8, tn), F32))
    o_ref[0] = jnp.sum(acc, axis=0, keepdims=True) + b_ref[0]


def _ada_mod(c_col, ada_w, ada_b3):
    depth, d, n = ada_w.shape
    tn = min(ADA_TN, n)
    return pl.pallas_call(
        _ada_kernel,
        grid=(depth, n // tn),
        in_specs=[
            pl.BlockSpec((d, 1), lambda l, j: (0, 0)),
            pl.BlockSpec((1, d, tn), lambda l, j: (l, 0, j)),
            pl.BlockSpec((1, 1, tn), lambda l, j: (l, 0, j)),
        ],
        out_specs=pl.BlockSpec((1, 1, tn), lambda l, j: (l, 0, j)),
        out_shape=jax.ShapeDtypeStruct((depth, 1, n), F32),
        scratch_shapes=[pltpu.VMEM((d, 1), F32)],
        compiler_params=_cparams(("arbitrary", "arbitrary")),
        name="ada_mod",
    )(c_col, ada_w, ada_b3)


def _rope_kernel(pos_row_ref, pos_col_ref, invf_col_ref, invf_row_ref, sgn_col_ref, sgn_row_ref,
                 cos_t_ref, sin_t_ref, cos4_ref, sin4_ref):
    ang_t = invf_col_ref[...] * pos_row_ref[...]
    cos_t_ref[...] = jnp.cos(ang_t)
    sin_t_ref[...] = jnp.sin(ang_t) * sgn_col_ref[...]
    ang4 = pos_col_ref[...] * invf_row_ref[...]
    cos4_ref[...] = jnp.cos(ang4)
    sin4_ref[...] = jnp.sin(ang4) * sgn_row_ref[...]


def _rope_tables(pos_f32):
    t = pos_f32.shape[0]
    tn = min(ROPE_TN, t)
    half = QK_ROPE_DIM // 2
    inv_freq = ROPE_THETA ** (-jnp.arange(half, dtype=F32) / half)
    invf_col = jnp.tile(inv_freq, 2).reshape(QK_ROPE_DIM, 1)
    invf_row = jnp.tile(inv_freq, LANES // half).reshape(1, LANES)
    sgn_col = jnp.asarray(np.repeat([-1.0, 1.0], half).reshape(QK_ROPE_DIM, 1), F32)
    sgn_row = jnp.asarray(np.tile(np.repeat([-1.0, 1.0], half), LANES // QK_ROPE_DIM).reshape(1, LANES), F32)
    const = lambda i: (0, 0)
    return pl.pallas_call(
        _rope_kernel,
        grid=(t // tn,),
        in_specs=[
            pl.BlockSpec((1, tn), lambda i: (0, i)),
            pl.BlockSpec((tn, 1), lambda i: (i, 0)),
            pl.BlockSpec((QK_ROPE_DIM, 1), const),
            pl.BlockSpec((1, LANES), const),
            pl.BlockSpec((QK_ROPE_DIM, 1), const),
            pl.BlockSpec((1, LANES), const),
        ],
        out_specs=[
            pl.BlockSpec((QK_ROPE_DIM, tn), lambda i: (0, i)),
            pl.BlockSpec((QK_ROPE_DIM, tn), lambda i: (0, i)),
            pl.BlockSpec((tn, LANES), lambda i: (i, 0)),
            pl.BlockSpec((tn, LANES), lambda i: (i, 0)),
        ],
        out_shape=[
            jax.ShapeDtypeStruct((QK_ROPE_DIM, t), F32),
            jax.ShapeDtypeStruct((QK_ROPE_DIM, t), F32),
            jax.ShapeDtypeStruct((t, LANES), F32),
            jax.ShapeDtypeStruct((t, LANES), F32),
        ],
        compiler_params=_cparams(("arbitrary",)),
        name="rope_tables",
    )(pos_f32.reshape(1, t), pos_f32.reshape(t, 1), invf_col, invf_row, sgn_col, sgn_row)


Q_SCALE = (QK_HEAD_DIM ** -0.5) * math.log2(math.e)


def _mix_in_kernel(x_ref, mod_ref, ng_ref, w_in_ref, w_kr_ref, cqg_ref, ckvg_ref, w_uq_ref, w_uv_ref, w_uk_ref,
                   qgn_ref, qgr_ref, qgs_ref, kgn_ref, kgr_ref, kgs_ref, cos4_ref, sin4_ref, cos_t_ref, sin_t_ref,
                   q_ref, kt_ref, v_ref, u_ref):
    d = x_ref.shape[1]
    x = x_ref[...]
    shift = mod_ref[:, 0:d]
    scale = mod_ref[:, d:2 * d]
    h = _rms(x, ng_ref[...]) * (1.0 + scale) + shift
    hb = h.astype(BF16)
    z = _dot(hb, w_in_ref[...])
    c_q = z[:, 0:Q_LORA_RANK]
    c_kv = z[:, Q_LORA_RANK:Q_LORA_RANK + KV_LORA_RANK]
    u_ref[...] = z[:, Q_LORA_RANK + KV_LORA_RANK:]
    cqn = _rms(c_q, cqg_ref[...]).astype(BF16)
    ckvn = _rms(c_kv, ckvg_ref[...]).astype(BF16)

    kn_t = lax.dot_general(w_uk_ref[...], ckvn, NT_DIMS, preferred_element_type=F32)
    kr_t = lax.dot_general(w_kr_ref[...], hb, NT_DIMS, preferred_element_type=F32)
    kr = kr_t[0:QK_ROPE_DIM]
    kr_rot = (kr * kgr_ref[...]) * cos_t_ref[...] + (kr_t[QK_ROPE_DIM:] * kgs_ref[...]) * sin_t_ref[...]
    ss_r = jnp.sum(kr * kr, axis=0, keepdims=True)
    tm = x.shape[0]
    pad_rows = jnp.zeros((QK_PAD - QK_HEAD_DIM, tm), BF16)
    for hd in range(MLA_HEADS):
        kn = kn_t[hd * QK_NOPE_DIM:(hd + 1) * QK_NOPE_DIM]
        ss = jnp.sum(kn * kn, axis=0, keepdims=True) + ss_r
        r = lax.rsqrt(ss * (1.0 / QK_HEAD_DIM) + RMS_EPS)
        kt_ref[hd, 0:QK_NOPE_DIM, :] = ((kn * r) * kgn_ref[...]).astype(BF16)
        kt_ref[hd, QK_NOPE_DIM:QK_HEAD_DIM, :] = (kr_rot * r).astype(BF16)
        kt_ref[hd, QK_HEAD_DIM:, :] = pad_rows

    vz = _dot(ckvn, w_uv_ref[...])
    ones = jnp.ones((tm, V_PAD - V_HEAD_DIM), BF16)
    for hd in range(MLA_HEADS):
        v_ref[hd, :, 0:V_HEAD_DIM] = vz[:, hd * V_HEAD_DIM:(hd + 1) * V_HEAD_DIM].astype(BF16)
        v_ref[hd, :, V_HEAD_DIM:] = ones

    qz = _dot(cqn, w_uq_ref[...])
    n_nope = MLA_HEADS * QK_NOPE_DIM
    n_rope = MLA_HEADS * QK_ROPE_DIM
    lane = lax.broadcasted_iota(jnp.int32, (x.shape[0], LANES), 1)
    for pair in range(MLA_HEADS // 2):
        pr = qz[:, n_nope + pair * LANES:n_nope + (pair + 1) * LANES]
        ps = qz[:, n_nope + n_rope + pair * LANES:n_nope + n_rope + (pair + 1) * LANES]
        rot = (pr * qgr_ref[...]) * cos4_ref[...] + (ps * qgs_ref[...]) * sin4_ref[...]
        pr2 = pr * pr
        for hd in (2 * pair, 2 * pair + 1):
            nope = qz[:, hd * QK_NOPE_DIM:(hd + 1) * QK_NOPE_DIM]
            own = (lane < QK_ROPE_DIM) if hd % 2 == 0 else (lane >= QK_ROPE_DIM)
            ss = jnp.sum(nope * nope + jnp.where(own, pr2, 0.0), axis=-1, keepdims=True)
            r = lax.rsqrt(ss * (1.0 / QK_HEAD_DIM) + RMS_EPS) * Q_SCALE
            q_ref[hd, :, 0:QK_NOPE_DIM] = ((nope * r) * qgn_ref[...]).astype(BF16)
            rope = rot * r
            if hd % 2 == 1:
                rope = pltpu.roll(rope, QK_ROPE_DIM, 1)
            q_ref[hd, :, QK_NOPE_DIM:] = jnp.where(lane < QK_ROPE_DIM, rope, 0.0).astype(BF16)


def _mix_in(x2d, mod_l, ng, wp, cos4, sin4, cos_t, sin_t):
    t, d = x2d.shape
    tm = min(MIX_TM, t)
    hh = MLA_HEADS
    const2 = lambda i: (0, 0)
    full = lambda a: pl.BlockSpec(a.shape, const2)
    ins = [x2d, mod_l, ng, wp["w_in"], wp["w_kr"], wp["cq_g"], wp["ckv_g"], wp["w_uq"], wp["w_uv"], wp["w_uk"],
           wp["qg_n"], wp["qg_r"], wp["qg_s"], wp["kg_n"], wp["kg_r"], wp["kg_s"]]
    in_specs = [pl.BlockSpec((tm, d), lambda i: (i, 0))] + [full(a) for a in ins[1:]]
    ins += [cos4, sin4, cos_t, sin_t]
    in_specs += [
        pl.BlockSpec((tm, LANES), lambda i: (i, 0)),
        pl.BlockSpec((tm, LANES), lambda i: (i, 0)),
        pl.BlockSpec((QK_ROPE_DIM, tm), lambda i: (0, i)),
        pl.BlockSpec((QK_ROPE_DIM, tm), lambda i: (0, i)),
    ]
    return pl.pallas_call(
        _mix_in_kernel,
        grid=(t // tm,),
        in_specs=in_specs,
        out_specs=[
            pl.BlockSpec((hh, tm, QK_PAD), lambda i: (0, i, 0)),
            pl.BlockSpec((hh, QK_PAD, tm), lambda i: (0, 0, i)),
            pl.BlockSpec((hh, tm, V_PAD), lambda i: (0, i, 0)),
            pl.BlockSpec((tm, FNET_WIDTH), lambda i: (i, 0)),
        ],
        out_shape=[
            jax.ShapeDtypeStruct((hh, t, QK_PAD), BF16),
            jax.ShapeDtypeStruct((hh, QK_PAD, t), BF16),
            jax.ShapeDtypeStruct((hh, t, V_PAD), BF16),
            jax.ShapeDtypeStruct((t, FNET_WIDTH), F32),
        ],
        compiler_params=_cparams(("arbitrary",)),
        name="mix_in",
    )(*ins)


ATT_FAST_BOUND = 48.0


def _score_bound(q_gain, k_gain):
    return (Q_SCALE * QK_HEAD_DIM) * jnp.max(jnp.abs(q_gain)) * jnp.max(jnp.abs(k_gain))


def _attn_kernel(fast_ref, q_ref, kt_ref, v_ref, *rest, tc, unroll, n_side):
    side_in = rest[:n_side]
    o_ref = rest[n_side]
    side_out = rest[n_side + 1:2 * n_side + 1]
    acc_ref, m_ref = rest[2 * n_side + 1:]
    n_chunks = kt_ref.shape[2] // tc
    acc_ref[...] = jnp.zeros(acc_ref.shape, F32)

    for wi_ref, wo_ref in zip(side_in, side_out):
        wo_ref[...] = wi_ref[...].astype(BF16)

    def chunk(c):
        off = pl.multiple_of(c * tc, tc)
        return kt_ref[0, :, pl.ds(off, tc)], v_ref[0, pl.ds(off, tc), :]

    @pl.when(fast_ref[0] == 1)
    def _():
        def body(c2, carry):
            for u in range(unroll):
                k_c, v_c = chunk(unroll * c2 + u)
                p = jnp.exp2(_dot(q_ref[0], k_c)).astype(BF16)
                acc_ref[...] += _dot(p, v_c)
            return carry

        lax.fori_loop(0, n_chunks // unroll, body, 0)

    @pl.when(fast_ref[0] != 1)
    def _():
        m_ref[...] = jnp.full(m_ref.shape, -jnp.inf, F32)

        def body(c, carry):
            k_c, v_c = chunk(c)
            s = _dot(q_ref[0], k_c)
            m_prev = m_ref[...]
            m_new = jnp.maximum(m_prev, jnp.max(s, axis=-1, keepdims=True))
            p = jnp.exp2(s - m_new).astype(BF16)
            acc_ref[...] = jnp.exp2(m_prev - m_new) * acc_ref[...] + _dot(p, v_c)
            m_ref[...] = m_new
            return carry

        lax.fori_loop(0, n_chunks, body, 0)

    o_ref[...] = (acc_ref[:, 0:V_HEAD_DIM] * (1.0 / acc_ref[:, V_HEAD_DIM:])).astype(o_ref.dtype)


def _attention(fast, q, kt, v, side=()):
    hh, t, _ = q.shape
    tq = min(ATT_TQ, t)
    tc = min(ATT_TC, t)
    unroll = min(ATT_UNROLL, t // tc)
    once = pl.Buffered(1)
    side_specs = [pl.BlockSpec(blk, (lambda h, i, f, im=im: im(h, i))) for _, blk, im in side]
    outs = pl.pallas_call(
        functools.partial(_attn_kernel, tc=tc, unroll=unroll, n_side=len(side)),
        grid_spec=pltpu.PrefetchScalarGridSpec(
            num_scalar_prefetch=1,
            grid=(hh, t // tq),
            in_specs=[
                pl.BlockSpec((1, tq, QK_PAD), lambda h, i, f: (h, i, 0)),
                pl.BlockSpec((1, QK_PAD, t), lambda h, i, f: (h, 0, 0), pipeline_mode=once),
                pl.BlockSpec((1, t, V_PAD), lambda h, i, f: (h, 0, 0), pipeline_mode=once),
            ] + side_specs,
            out_specs=[pl.BlockSpec((tq, V_HEAD_DIM), lambda h, i, f: (i, h))] + side_specs,
            scratch_shapes=[pltpu.VMEM((tq, V_PAD), F32), pltpu.VMEM((tq, 1), F32)],
        ),
        out_shape=[jax.ShapeDtypeStruct((t, hh * V_HEAD_DIM), BF16)]
        + [jax.ShapeDtypeStruct(w.shape, BF16) for w, _, _ in side],
        compiler_params=_cparams(("arbitrary", "arbitrary"), vmem=ATT_VMEM_LIMIT_BYTES),
        name="attention",
    )(fast, q, kt, v, *[w for w, _, _ in side])
    return outs


def _fnet1_kernel(u_ref, f1_ref, twc_ref, tws_ref, zr_ref, zi_ref):
    k1, n2, width = zr_ref.shape
    y = _dot(f1_ref[...], u_ref[...].astype(BF16))
    a = y[0:n2]
    b = y[n2:]
    for i in range(k1):
        ai = a[:, i * width:(i + 1) * width]
        bi = b[:, i * width:(i + 1) * width]
        c = twc_ref[i]
        s = tws_ref[i]
        zr_ref[i] = (ai * c - bi * s).astype(zr_ref.dtype)
        zi_ref[i] = (-(bi * c) - ai * s).astype(zi_ref.dtype)


def _fnet2_kernel(zr_ref, zi_ref, f2_ref, cs_ref, wf_ref, o_ref):
    n1 = zr_ref.shape[0]
    k2 = zr_ref.shape[1] // FNET_WIDTH
    gd = FNET_GROUP_DIM
    zz = jnp.concatenate([zr_ref[...], zi_ref[...]], axis=0)
    x = _dot(f2_ref[...], zz)
    xr = x[0:n1]
    xi = x[n1:]
    for g in range(FNET_GROUPS):
        rows = []
        for j in range(k2):
            c0 = j * FNET_WIDTH + g * gd
            rows.append(jnp.concatenate([xr[:, c0:c0 + gd], xi[:, c0:c0 + gd]], axis=1))
        lhs = jnp.concatenate(rows, axis=0).astype(BF16)
        fg = _dot(lhs, cs_ref[...]).astype(BF16)
        og = _dot(fg, wf_ref[g])
        for j in range(k2):
            c0 = j * FNET_WIDTH + g * gd
            o_ref[:, c0:c0 + gd] = og[j * n1:(j + 1) * n1].astype(o_ref.dtype)


def _dft_consts(n1, n2):
    n = n1 * n2
    a2 = 2.0 * np.pi * np.outer(np.arange(n2), np.arange(n2)) / n2
    f1 = np.concatenate([np.cos(a2), np.sin(a2)], axis=0) / n2
    atw = 2.0 * np.pi * np.outer(np.arange(n1), np.arange(n2)) / n
    a1 = 2.0 * np.pi * np.outer(np.arange(n1), np.arange(n1)) / n1
    c1, s1 = np.cos(a1), np.sin(a1)
    f2 = np.block([[c1, s1], [-s1, c1]])
    ac = 2.0 * np.pi * np.outer(np.arange(FNET_GROUP_DIM), np.arange(FNET_GROUP_DIM)) / FNET_GROUP_DIM
    norm = n2 / math.sqrt(n * FNET_GROUP_DIM)
    cs = np.concatenate([np.cos(ac), np.sin(ac)], axis=0) * norm
    return (jnp.asarray(f1, BF16), jnp.asarray(np.cos(atw)[:, :, None], F32), jnp.asarray(np.sin(atw)[:, :, None], F32),
            jnp.asarray(f2, BF16), jnp.asarray(cs, BF16))


def _fnet(u, wf_b, n1, n2):
    t, width = u.shape
    assert t == n1 * n2
    f1, twc, tws, f2, cs = _dft_consts(n1, n2)
    k1 = min(FNET_K1, n1)
    k2 = min(FNET_K2, n2)
    const2 = lambda i: (0, 0)
    zr, zi = pl.pallas_call(
        _fnet1_kernel,
        grid=(n1 // k1,),
        in_specs=[
            pl.BlockSpec((n2, k1 * width), lambda i: (0, i)),
            pl.BlockSpec(f1.shape, const2),
            pl.BlockSpec((k1, n2, 1), lambda i: (i, 0, 0)),
            pl.BlockSpec((k1, n2, 1), lambda i: (i, 0, 0)),
        ],
        out_specs=[pl.BlockSpec((k1, n2, width), lambda i: (i, 0, 0))] * 2,
        out_shape=[jax.ShapeDtypeStruct((n1, n2, width), BF16)] * 2,
        compiler_params=_cparams(("arbitrary",)),
        name="fnet_stage1",
    )(u.reshape(n2, n1 * width), f1, twc, tws)
    out = pl.pallas_call(
        _fnet2_kernel,
        grid=(n2 // k2,),
        in_specs=[
            pl.BlockSpec((n1, k2 * width), lambda i: (0, i)),
            pl.BlockSpec((n1, k2 * width), lambda i: (0, i)),
            pl.BlockSpec(f2.shape, const2),
            pl.BlockSpec(cs.shape, const2),
            pl.BlockSpec(wf_b.shape, lambda i: (0, 0, 0)),
        ],
        out_specs=pl.BlockSpec((n1, k2 * width), lambda i: (0, i)),
        out_shape=jax.ShapeDtypeStruct((n1, n2 * width), BF16),
        compiler_params=_cparams(("arbitrary",)),
        name="fnet_stage2",
    )(zr.reshape(n1, n2 * width), zi.reshape(n1, n2 * width), f2, cs, wf_b)
    return out.reshape(t, width)


def _mix_out_kernel(a_ref, f_ref, x_ref, mod_ref, og_ref, w_out_ref, fg_ref, *rest, moe):
    d = x_ref.shape[1]
    half = a_ref.shape[1]
    an = _rms(a_ref[...].astype(F32), og_ref[:, 0:half])
    fn = _rms(f_ref[...].astype(F32), og_ref[:, half:])
    cat = jnp.concatenate([an, fn], axis=1).astype(BF16)
    o = _dot(cat, w_out_ref[...])
    gate1 = mod_ref[:, 2 * d:3 * d]
    shift2 = mod_ref[:, 3 * d:4 * d]
    scale2 = mod_ref[:, 4 * d:5 * d]
    x1 = x_ref[...] + gate1 * o
    h2 = _rms(x1, fg_ref[...]) * (1.0 + scale2) + shift2
    if not moe:
        x1_ref, h2_ref = rest
        x1_ref[...] = x1
        h2_ref[...] = h2.astype(h2_ref.dtype)
        return
    wr_ref, tri_ref, x1_ref, h2_ref, route_ref, cnt_ref, carry_ref = rest
    x1_ref[...] = x1
    h2_ref[...] = h2

    @pl.when(pl.program_id(0) == 0)
    def _():
        carry_ref[...] = jnp.zeros(carry_ref.shape, F32)

    tm = x1.shape[0]
    h_hi = h2.astype(BF16)
    h_lo = (h2 - h_hi.astype(F32)).astype(BF16)
    hw = _dot(h_hi, wr_ref[...])
    logits = (hw[:, 0:LANES] + hw[:, LANES:]) + _dot(h_lo, wr_ref[:, 0:LANES])
    lane = lax.broadcasted_iota(jnp.int32, (tm, LANES), 1)
    neg = jnp.float32(-jnp.inf)
    logits = jnp.where(lane < N_EXPERTS, logits, neg)
    m1 = jnp.max(logits, axis=-1, keepdims=True)
    i1 = jnp.min(jnp.where(logits == m1, lane, LANES), axis=-1, keepdims=True)
    rest_l = jnp.where(lane == i1, neg, logits)
    m2 = jnp.max(rest_l, axis=-1, keepdims=True)
    i2 = jnp.min(jnp.where(rest_l == m2, lane, LANES), axis=-1, keepdims=True)
    e21 = jnp.exp(m2 - m1)
    w1 = 1.0 / (1.0 + e21)
    w2 = e21 / (1.0 + e21)
    oh1 = jnp.where(lane == i1, 1.0, 0.0)
    oh2 = jnp.where(lane == i2, 1.0, 0.0)
    pre = _dot(tri_ref[...], jnp.concatenate([oh1, oh2], axis=1).astype(BF16))
    cnt1 = jnp.sum(oh1, axis=0, keepdims=True)
    cnt2 = jnp.sum(oh2, axis=0, keepdims=True)
    carry = carry_ref[0:1, :]
    rank1 = jnp.sum(oh1 * (carry + pre[:, 0:LANES]), axis=-1, keepdims=True)
    rank2 = jnp.sum(oh2 * (carry + cnt1 + pre[:, LANES:]), axis=-1, keepdims=True)
    new_carry = carry + cnt1 + cnt2
    carry_ref[...] = jnp.broadcast_to(new_carry, carry_ref.shape)
    cnt_ref[...] = jnp.broadcast_to(new_carry, cnt_ref.shape)
    vals = (i1.astype(F32), i2.astype(F32), w1, w2, rank1, rank2)
    route = jnp.zeros((tm, LANES), F32)
    for k, val in enumerate(vals):
        route = jnp.where(lane == k, val, route)
    route_ref[...] = route


def _mix_out(a, f, x2d, mod_l, og, w_out_b, fg, router=None):
    t, d = x2d.shape
    tm = min(MIX_TM, t)
    half = a.shape[1]
    moe = router is not None
    const2 = lambda i: (0, 0)
    row = lambda w: pl.BlockSpec((tm, w), lambda i: (i, 0))
    ins = [a, f, x2d, mod_l, og, w_out_b, fg]
    in_specs = [row(half), row(half), row(d), pl.BlockSpec(mod_l.shape, const2), pl.BlockSpec(og.shape, const2),
                pl.BlockSpec(w_out_b.shape, const2), pl.BlockSpec(fg.shape, const2)]
    out_specs = [row(d), row(d)]
    out_shape = [jax.ShapeDtypeStruct((t, d), F32), jax.ShapeDtypeStruct((t, d), F32 if moe else BF16)]
    scratch = []
    if moe:
        tri = jnp.asarray(np.tril(np.ones((tm, tm), np.float32), -1), BF16)
        ins += [router, tri]
        in_specs += [pl.BlockSpec(router.shape, const2), pl.BlockSpec(tri.shape, const2)]
        out_specs += [row(LANES), pl.BlockSpec((8, LANES), const2)]
        out_shape += [jax.ShapeDtypeStruct((t, LANES), F32), jax.ShapeDtypeStruct((8, LANES), F32)]
        scratch = [pltpu.VMEM((8, LANES), F32)]
    return pl.pallas_call(
        functools.partial(_mix_out_kernel, moe=moe),
        grid=(t // tm,),
        in_specs=in_specs,
        out_specs=out_specs,
        out_shape=out_shape,
        scratch_shapes=scratch,
        compiler_params=_cparams(("arbitrary",)),
        name="mix_out_moe" if moe else "mix_out",
    )(*ins)


def _silu(g):
    return g * (1.0 / (1.0 + jnp.exp(-g)))


def _ffn_kernel(h_ref, x1_ref, mod_ref, wg_ref, wu_ref, wd_ref, o_ref):
    j = pl.program_id(1)
    d = x1_ref.shape[1]
    @pl.when(j == 0)
    def _():
        o_ref[...] = jnp.zeros(o_ref.shape, F32)

    h = h_ref[...]
    act = (_silu(_dot(h, wg_ref[...])) * _dot(h, wu_ref[...])).astype(BF16)
    o_ref[...] += _dot(act, wd_ref[...])

    @pl.when(j == pl.num_programs(1) - 1)
    def _():
        o_ref[...] = x1_ref[...] + mod_ref[:, 5 * d:6 * d] * o_ref[...]


def _ffn_dense(h2, x1, mod_l, wg, wu, wd):
    t, d = x1.shape
    ff = wg.shape[1]
    tm = min(FFN_TM, t)
    tf = min(FFN_TF, ff)
    return pl.pallas_call(
        _ffn_kernel,
        grid=(t // tm, ff // tf),
        in_specs=[
            pl.BlockSpec((tm, d), lambda i, j: (i, 0)),
            pl.BlockSpec((tm, d), lambda i, j: (i, 0)),
            pl.BlockSpec(mod_l.shape, lambda i, j: (0, 0)),
            pl.BlockSpec((d, tf), lambda i, j: (0, j)),
            pl.BlockSpec((d, tf), lambda i, j: (0, j)),
            pl.BlockSpec((tf, d), lambda i, j: (j, 0)),
        ],
        out_specs=pl.BlockSpec((tm, d), lambda i, j: (i, 0)),
        out_shape=jax.ShapeDtypeStruct((t, d), F32),
        compiler_params=_cparams(("arbitrary", "arbitrary")),
        name="ffn_dense",
    )(h2, x1, mod_l, wg, wu, wd)


def _dispatch_kernel(d1_ref, d2_ref, tail_ref, h_ref, xs_ref, zero_ref, sem, zsem):
    tm = h_ref.shape[0]
    base = pl.program_id(0) * tm
    blk = zero_ref.shape[0]

    @pl.when(pl.program_id(0) == 0)
    def _():
        zero_ref[...] = jnp.zeros(zero_ref.shape, zero_ref.dtype)

        def fill(e):
            return pltpu.make_async_copy(zero_ref, xs_ref.at[pl.ds(pl.multiple_of(tail_ref[e], blk), blk), :], zsem)

        for e in range(tail_ref.shape[0]):
            @pl.when(tail_ref[e] >= 0)
            def _():
                fill(e).start()

        for e in range(tail_ref.shape[0]):
            @pl.when(tail_ref[e] >= 0)
            def _():
                fill(e).wait()

    def copy(r, dst):
        return pltpu.make_async_copy(h_ref.at[pl.ds(r, 1), :], xs_ref.at[pl.ds(dst, 1), :], sem)

    def start(r, c):
        copy(r, d1_ref[base + r]).start()
        copy(r, d2_ref[base + r]).start()
        return c

    def wait(r, c):
        copy(r, d1_ref[base + r]).wait()
        copy(r, d2_ref[base + r]).wait()
        return c

    lax.fori_loop(0, tm, start, 0, unroll=ROW_DMA_UNROLL)
    lax.fori_loop(0, tm, wait, 0, unroll=ROW_DMA_UNROLL)


def _dispatch(dest1, dest2, tail, h2, n_rows, blk):
    t, d = h2.shape
    tm = min(ROW_TM, t)
    return pl.pallas_call(
        _dispatch_kernel,
        grid_spec=pltpu.PrefetchScalarGridSpec(
            num_scalar_prefetch=3,
            grid=(t // tm,),
            in_specs=[pl.BlockSpec((tm, d), lambda i, d1, d2, tl: (i, 0))],
            out_specs=pl.BlockSpec(memory_space=pl.ANY),
            scratch_shapes=[pltpu.VMEM((blk, d), h2.dtype), pltpu.SemaphoreType.DMA(()), pltpu.SemaphoreType.DMA(())],
        ),
        out_shape=jax.ShapeDtypeStruct((n_rows, d), h2.dtype),
        compiler_params=_cparams(("arbitrary",)),
        name="moe_dispatch",
    )(dest1, dest2, tail, h2)


def _expert_kernel(be_ref, nu_ref, x_ref, wg_ref, wu_ref, wd_ref, o_ref, xb_ref):
    b = pl.program_id(0)
    j = pl.program_id(1)

    @pl.when(j == 0)
    def _():
        o_ref[...] = jnp.zeros(o_ref.shape, o_ref.dtype)
        xb_ref[...] = x_ref[...].astype(BF16)

    @pl.when(b < nu_ref[0])
    def _():
        xb = xb_ref[...]
        act = (_silu(_dot(xb, wg_ref[0])) * _dot(xb, wu_ref[0])).astype(BF16)
        o_ref[...] += _dot(act, wd_ref[0])


def _experts(block_e, n_used, xs, wg, wu, wd):
    n_rows, d = xs.shape
    ne, _, ff = wg.shape
    blk = min(MOE_BLK, n_rows)
    tf = min(MOE_TF, ff)
    nb = n_rows // blk
    nj = ff // tf

    def last_live(b, nu):
        return jnp.maximum(jnp.minimum(b, nu[0] - 1), 0)

    def row_map(b, j, be, nu):
        return (last_live(b, nu), 0)

    def col_map(b, j, be, nu):
        return (be[last_live(b, nu)], 0, jnp.where(b < nu[0], j, nj - 1))

    def down_map(b, j, be, nu):
        return (be[last_live(b, nu)], jnp.where(b < nu[0], j, nj - 1), 0)

    return pl.pallas_call(
        _expert_kernel,
        grid_spec=pltpu.PrefetchScalarGridSpec(
            num_scalar_prefetch=2,
            grid=(nb, nj),
            in_specs=[
                pl.BlockSpec((blk, d), row_map),
                pl.BlockSpec((1, d, tf), col_map),
                pl.BlockSpec((1, d, tf), col_map),
                pl.BlockSpec((1, tf, d), down_map),
            ],
            out_specs=pl.BlockSpec((blk, d), lambda b, j, be, nu: (b, 0)),
            scratch_shapes=[pltpu.VMEM((blk, d), BF16)],
        ),
        out_shape=jax.ShapeDtypeStruct((n_rows, d), F32),
        compiler_params=_cparams(("arbitrary", "arbitrary")),
        name="moe_experts",
    )(block_e, n_used, xs, wg, wu, wd)


def _combine_kernel(d1_ref, d2_ref, yb_ref, route_ref, x1_ref, mod_ref, o_ref, buf_ref, sem):
    tm, d = x1_ref.shape
    base = pl.program_id(0) * tm

    def copy(slot, r, src):
        return pltpu.make_async_copy(yb_ref.at[pl.ds(src, 1), :], buf_ref.at[slot, pl.ds(r, 1), :], sem)

    def start(r, c):
        copy(0, r, d1_ref[base + r]).start()
        copy(1, r, d2_ref[base + r]).start()
        return c

    def wait(r, c):
        copy(0, r, d1_ref[base + r]).wait()
        copy(1, r, d2_ref[base + r]).wait()
        return c

    lax.fori_loop(0, tm, start, 0, unroll=ROW_DMA_UNROLL)
    lax.fori_loop(0, tm, wait, 0, unroll=ROW_DMA_UNROLL)
    w1 = route_ref[:, 2:3]
    w2 = route_ref[:, 3:4]
    y = buf_ref[0] * w1 + buf_ref[1] * w2
    o_ref[...] = x1_ref[...] + mod_ref[:, 5 * d:6 * d] * y


def _combine(dest1, dest2, yb, route, x1, mod_l):
    t, d = x1.shape
    tm = min(ROW_TM, t)
    row = lambda w: pl.BlockSpec((tm, w), lambda i, d1, d2: (i, 0))
    return pl.pallas_call(
        _combine_kernel,
        grid_spec=pltpu.PrefetchScalarGridSpec(
            num_scalar_prefetch=2,
            grid=(t // tm,),
            in_specs=[pl.BlockSpec(memory_space=pl.ANY), row(LANES), row(d),
                      pl.BlockSpec(mod_l.shape, lambda i, d1, d2: (0, 0))],
            out_specs=row(d),
            scratch_shapes=[pltpu.VMEM((2, tm, d), F32), pltpu.SemaphoreType.DMA(())],
        ),
        out_shape=jax.ShapeDtypeStruct((t, d), F32),
        compiler_params=_cparams(("arbitrary",)),
        name="moe_combine",
    )(dest1, dest2, yb, route, x1, mod_l)


def _moe_plan(route, cnt, t, blk):
    e1 = route[:, 0].astype(jnp.int32)
    e2 = route[:, 1].astype(jnp.int32)
    r1 = route[:, 4].astype(jnp.int32)
    r2 = route[:, 5].astype(jnp.int32)
    counts = cnt[0, :N_EXPERTS].astype(jnp.int32)
    padded = (counts + blk - 1) // blk * blk
    pend = jnp.cumsum(padded)
    pstart = pend - padded
    eids = jnp.arange(N_EXPERTS, dtype=jnp.int32)
    dest1 = jnp.sum(jnp.where(e1[:, None] == eids[None, :], pstart[None, :], 0), axis=1) + r1
    dest2 = jnp.sum(jnp.where(e2[:, None] == eids[None, :], pstart[None, :], 0), axis=1) + r2
    n_blocks = -(-(2 * t) // blk) + N_EXPERTS
    bstart = jnp.arange(n_blocks, dtype=jnp.int32) * blk
    block_e = jnp.minimum(jnp.sum((pend[None, :] <= bstart[:, None]).astype(jnp.int32), axis=1), N_EXPERTS - 1)
    n_used = (pend[-1] // blk).astype(jnp.int32).reshape(1)
    tail = jnp.where(padded > 0, pend - blk, -1)
    spare = bstart[(2 * t) // blk:]
    tail = jnp.concatenate([tail, jnp.where(spare >= pend[-1], spare, -1)]).astype(jnp.int32)
    return (dest1.astype(jnp.int32), dest2.astype(jnp.int32), block_e.astype(jnp.int32), n_used, tail,
            n_blocks * blk)


def _prep_mixer_weights(w_in, cq_norm, ckv_norm, w_uq, w_ukv, q_norm, k_norm):
    s1 = Q_LORA_RANK
    s2 = s1 + KV_LORA_RANK
    s3 = s2 + QK_ROPE_DIM
    half = QK_ROPE_DIM // 2
    swap = np.concatenate([np.arange(half, QK_ROPE_DIM), np.arange(half)])
    w_kr = w_in[:, s2:s3]
    uq = w_uq.reshape(Q_LORA_RANK, MLA_HEADS, QK_HEAD_DIM)
    uq_rope = uq[:, :, QK_NOPE_DIM:]
    ukv = w_ukv.reshape(KV_LORA_RANK, MLA_HEADS, QK_NOPE_DIM + V_HEAD_DIM)
    qg_r = q_norm[QK_NOPE_DIM:]
    kg_r = k_norm[QK_NOPE_DIM:]
    reps = LANES // QK_ROPE_DIM
    return {
        "w_in": jnp.concatenate([w_in[:, :s2], w_in[:, s3:]], axis=1).astype(BF16),
        "w_kr": jnp.concatenate([w_kr, w_kr[:, swap]], axis=1).T.astype(BF16),
        "cq_g": cq_norm.reshape(1, -1),
        "ckv_g": ckv_norm.reshape(1, -1),
        "w_uq": jnp.concatenate([uq[:, :, :QK_NOPE_DIM].reshape(Q_LORA_RANK, -1),
                                 uq_rope.reshape(Q_LORA_RANK, -1),
                                 uq_rope[:, :, swap].reshape(Q_LORA_RANK, -1)], axis=1).astype(BF16),
        "w_uv": ukv[:, :, QK_NOPE_DIM:].reshape(KV_LORA_RANK, -1).astype(BF16),
        "w_uk": ukv[:, :, :QK_NOPE_DIM].reshape(KV_LORA_RANK, -1).T.astype(BF16),
        "qg_n": q_norm[:QK_NOPE_DIM].reshape(1, -1),
        "qg_r": jnp.tile(qg_r, reps).reshape(1, -1),
        "qg_s": jnp.tile(qg_r[swap], reps).reshape(1, -1),
        "kg_n": k_norm[:QK_NOPE_DIM].reshape(-1, 1),
        "kg_r": kg_r.reshape(-1, 1),
        "kg_s": kg_r[swap].reshape(-1, 1),
    }


BF16_ROWS = 16


def _row_blocked(w, n_qt):
    steps = MLA_HEADS * n_qt
    tiles = w.shape[0] // BF16_ROWS
    n_blk = max(k for k in range(1, steps + 1) if tiles % k == 0)
    return w, (w.shape[0] // n_blk, w.shape[1]), lambda h, i: (jnp.minimum(h * n_qt + i, n_blk - 1), 0)


def _expert_blocked(w, n_qt):
    assert w.shape[0] == MLA_HEADS and w.shape[1] % (n_qt * BF16_ROWS) == 0
    return w, (1, w.shape[1] // n_qt, w.shape[2]), lambda h, i: (h, i, 0)


def _fnet_factors(t):
    n2 = 1 << (int(math.log2(t)) // 2)
    return t // n2, n2


def kernel(x, c, positions, ada_w, ada_b, mix_norm, w_in, cq_norm, ckv_norm, w_uq, w_ukv, q_norm, k_norm, w_fnet,
           out_norm, w_out, ffn_norm, dense_w_gate, dense_w_up, dense_w_down, router_w, moe_w_gate, moe_w_up,
           moe_w_down):
    b, s, d = x.shape
    assert b == 1
    t = b * s
    depth = ada_w.shape[0]
    assert depth == 2 and dense_w_gate.shape[0] == 1 and moe_w_gate.shape[0] == 1
    x2d = x.reshape(t, d)
    mod = _ada_mod(c.reshape(d, 1), ada_w, ada_b.reshape(depth, 1, -1))
    cos_t, sin_t, cos4, sin4 = _rope_tables(positions.reshape(t).astype(F32))
    n1, n2 = _fnet_factors(t)
    n_qt = t // min(ATT_TQ, t)
    side_jobs = [
        [_row_blocked(dense_w_gate[0], n_qt), _row_blocked(dense_w_up[0], n_qt), _row_blocked(dense_w_down[0], n_qt),
         _expert_blocked(moe_w_gate[0], n_qt)],
        [_expert_blocked(moe_w_up[0], n_qt), _expert_blocked(moe_w_down[0], n_qt)],
    ]
    cast = []
    for l in range(depth):
        mod_l = mod[l]
        wp = _prep_mixer_weights(w_in[l], cq_norm[l], ckv_norm[l], w_uq[l], w_ukv[l], q_norm[l], k_norm[l])
        fast = (_score_bound(q_norm[l], k_norm[l]) <= ATT_FAST_BOUND).astype(jnp.int32).reshape(1)
        q, kt, v, u = _mix_in(x2d, mod_l, mix_norm[l].reshape(1, d), wp, cos4, sin4, cos_t, sin_t)
        a, *new_cast = _attention(fast, q, kt, v, side_jobs[l])
        cast += new_cast
        f = _fnet(u, w_fnet[l].astype(BF16), n1, n2)
        og = out_norm[l].reshape(1, d)
        fg = ffn_norm[l].reshape(1, d)
        w_out_b = w_out[l].astype(BF16)
        j = l // 2
        if l % 2 == 0:
            x1, h2 = _mix_out(a, f, x2d, mod_l, og, w_out_b, fg)
            x2d = _ffn_dense(h2, x1, mod_l, cast[0], cast[1], cast[2])
        else:
            wr = jnp.pad(router_w[j], ((0, 0), (0, LANES - N_EXPERTS)))
            wr_hi = wr.astype(BF16)
            router = jnp.concatenate([wr_hi, (wr - wr_hi.astype(F32)).astype(BF16)], axis=1)
            x1, h2, route, cnt = _mix_out(a, f, x2d, mod_l, og, w_out_b, fg, router=router)
            dest1, dest2, block_e, n_used, tail, n_rows = _moe_plan(route, cnt, t, MOE_BLK)
            xs = _dispatch(dest1, dest2, tail, h2, n_rows, MOE_BLK)
            yb = _experts(block_e, n_used, xs, cast[3], cast[4], cast[5])
            x2d = _combine(dest1, dest2, yb, route, x1, mod_l)
    return x2d.reshape(b, s, d)
```

```python
import functools
import math

import numpy as np
import jax
import jax.numpy as jnp
from jax import lax
from jax.experimental import pallas as pl
from jax.experimental.pallas import tpu as pltpu

D_MODEL = 2048
DEPTH = 2
MLA_HEADS = 8
QK_NOPE_DIM = 128
QK_ROPE_DIM = 64
V_HEAD_DIM = 128
QK_HEAD_DIM = QK_NOPE_DIM + QK_ROPE_DIM
Q_LORA_RANK = 512
KV_LORA_RANK = 256
MLA_WIDTH = MLA_HEADS * V_HEAD_DIM
FNET_GROUPS = 8
FNET_GROUP_DIM = 128
FNET_WIDTH = FNET_GROUPS * FNET_GROUP_DIM
ROPE_THETA = 10000.0
N_EXPERTS = 8
N_MOD = 6
RMS_EPS = 1e-6

LANES = 128
QK_PAD = 256
V_PAD = 256
VMEM_LIMIT_BYTES = 56 * 1024 * 1024
ATT_VMEM_LIMIT_BYTES = 60 * 1024 * 1024

ADA_TN = 1024
ROPE_TN = 2048
MIX_TM = 512
ATT_TQ = 1024
ATT_TC = 1024
ATT_UNROLL = 8
FNET_K1 = 4
FNET_K2 = 8
FFN_TM = 512
FFN_TF = 512
MOE_BLK = 512
MOE_TF = 1024
ROW_TM = 256
ROW_DMA_UNROLL = 8

BF16 = jnp.bfloat16
F32 = jnp.float32
NT_DIMS = (((1,), (1,)), ((), ()))


def _cparams(sem, vmem=VMEM_LIMIT_BYTES):
    return pltpu.CompilerParams(dimension_semantics=sem, vmem_limit_bytes=vmem)


def _dot(a, b):
    return jnp.dot(a, b, preferred_element_type=F32)


def _rms(x, g):
    ms = jnp.mean(x * x, axis=-1, keepdims=True)
    return (x * lax.rsqrt(ms + RMS_EPS)) * g


def _ada_kernel(c_ref, w_ref, b_ref, o_ref, cond_ref):
    d = w_ref.shape[1]
    tn = w_ref.shape[2]
    rows = 64

    @pl.when(jnp.logical_and(pl.program_id(0) == 0, pl.program_id(1) == 0))
    def _():
        c = c_ref[...]
        cond_ref[...] = jnp.broadcast_to(c * (1.0 / (1.0 + jnp.exp(-c))), cond_ref.shape)

    def body(i, acc):
        r = pl.multiple_of(i * rows, rows)
        cond = cond_ref[pl.ds(r, rows), :]
        for s in range(rows // 8):
            cs = cond[8 * s:8 * s + 8]
            w8 = w_ref[0, pl.ds(r + 8 * s, 8), :]
            acc = acc + w8 * jnp.concatenate([cs] * (tn // LANES), axis=1)
        return acc

    acc = lax.fori_loop(0, d // rows, body, jnp.zeros((8, tn), F32))
    o_ref[0] = jnp.sum(acc, axis=0, keepdims=True) + b_ref[0]


def _ada_mod(c_col, ada_w, ada_b3):
    depth, d, n = ada_w.shape
    tn = min(ADA_TN, n)
    return pl.pallas_call(
        _ada_kernel,
        grid=(depth, n // tn),
        in_specs=[
            pl.BlockSpec((d, 1), lambda l, j: (0, 0)),
            pl.BlockSpec((1, d, tn), lambda l, j: (l, 0, j)),
            pl.BlockSpec((1, 1, tn), lambda l, j: (l, 0, j)),
        ],
        out_specs=pl.BlockSpec((1, 1, tn), lambda l, j: (l, 0, j)),
        out_shape=jax.ShapeDtypeStruct((depth, 1, n), F32),
        scratch_shapes=[pltpu.VMEM((d, LANES), F32)],
        compiler_params=_cparams(("arbitrary", "arbitrary")),
        name="ada_mod",
    )(c_col, ada_w, ada_b3)


def _rope_kernel(pos_ref, invf_ref, cos_ref, sin_ref):
    ang = invf_ref[...] * pos_ref[...]
    cos_ref[...] = jnp.cos(ang)
    sin_ref[...] = jnp.sin(ang)


def _rope_tables(pos_f32):
    t = pos_f32.shape[0]
    tn = min(ROPE_TN, t)
    half = QK_ROPE_DIM // 2
    inv_freq = ROPE_THETA ** (-jnp.arange(half, dtype=F32) / half)
    return pl.pallas_call(
        _rope_kernel,
        grid=(t // tn,),
        in_specs=[pl.BlockSpec((1, tn), lambda i: (0, i)), pl.BlockSpec((half, 1), lambda i: (0, 0))],
        out_specs=[pl.BlockSpec((half, tn), lambda i: (0, i))] * 2,
        out_shape=[jax.ShapeDtypeStruct((half, t), F32)] * 2,
        compiler_params=_cparams(("arbitrary",)),
        name="rope_tables",
    )(pos_f32.reshape(1, t), inv_freq.reshape(half, 1))


Q_SCALE = (QK_HEAD_DIM ** -0.5) * math.log2(math.e)


def _mix_in_kernel(x_ref, mod_ref, ng_ref, w_in_ref, w_kr_ref, cqg_ref, ckvg_ref, w_uq_ref, w_uv_ref, w_uk_ref,
                   qgn_ref, qgr_ref, qgs_ref, kgn_ref, kgr_ref, kgs_ref, cos_ref, sin_ref,
                   q_ref, kt_ref, v_ref, u_ref):
    d = x_ref.shape[1]
    cos = cos_ref[...]
    sin = sin_ref[...]
    cos_t = jnp.concatenate([cos, cos], axis=0)
    sin_t = jnp.concatenate([-sin, sin], axis=0)
    cos4 = jnp.concatenate([cos_t, cos_t], axis=0).T
    sin4 = jnp.concatenate([sin_t, sin_t], axis=0).T
    x = x_ref[...]
    shift = mod_ref[:, 0:d]
    scale = mod_ref[:, d:2 * d]
    h = _rms(x, ng_ref[...]) * (1.0 + scale) + shift
    hb = h.astype(BF16)
    z = _dot(hb, w_in_ref[...])
    c_q = z[:, 0:Q_LORA_RANK]
    c_kv = z[:, Q_LORA_RANK:Q_LORA_RANK + KV_LORA_RANK]
    u_ref[...] = z[:, Q_LORA_RANK + KV_LORA_RANK:]
    cqn = _rms(c_q, cqg_ref[...]).astype(BF16)
    ckvn = _rms(c_kv, ckvg_ref[...]).astype(BF16)

    kn_t = lax.dot_general(w_uk_ref[...], ckvn, NT_DIMS, preferred_element_type=F32)
    kr_t = lax.dot_general(w_kr_ref[...], hb, NT_DIMS, preferred_element_type=F32)
    kr = kr_t[0:QK_ROPE_DIM]
    kr_rot = (kr * kgr_ref[...]) * cos_t + (kr_t[QK_ROPE_DIM:] * kgs_ref[...]) * sin_t
    ss_r = jnp.sum(kr * kr, axis=0, keepdims=True)
    tm = x.shape[0]
    pad_rows = jnp.zeros((QK_PAD - QK_HEAD_DIM, tm), BF16)
    for hd in range(MLA_HEADS):
        kn = kn_t[hd * QK_NOPE_DIM:(hd + 1) * QK_NOPE_DIM]
        ss = jnp.sum(kn * kn, axis=0, keepdims=True) + ss_r
        r = lax.rsqrt(ss * (1.0 / QK_HEAD_DIM) + RMS_EPS)
        kt_ref[hd, 0:QK_NOPE_DIM, :] = ((kn * r) * kgn_ref[...]).astype(BF16)
        kt_ref[hd, QK_NOPE_DIM:QK_HEAD_DIM, :] = (kr_rot * r).astype(BF16)
        kt_ref[hd, QK_HEAD_DIM:, :] = pad_rows

    vz = _dot(ckvn, w_uv_ref[...])
    ones = jnp.ones((tm, V_PAD - V_HEAD_DIM), BF16)
    for hd in range(MLA_HEADS):
        v_ref[hd, :, 0:V_HEAD_DIM] = vz[:, hd * V_HEAD_DIM:(hd + 1) * V_HEAD_DIM].astype(BF16)
        v_ref[hd, :, V_HEAD_DIM:] = ones

    qz = _dot(cqn, w_uq_ref[...])
    n_nope = MLA_HEADS * QK_NOPE_DIM
    n_rope = MLA_HEADS * QK_ROPE_DIM
    lane = lax.broadcasted_iota(jnp.int32, (x.shape[0], LANES), 1)
    for pair in range(MLA_HEADS // 2):
        pr = qz[:, n_nope + pair * LANES:n_nope + (pair + 1) * LANES]
        ps = qz[:, n_nope + n_rope + pair * LANES:n_nope + n_rope + (pair + 1) * LANES]
        rot = (pr * qgr_ref[...]) * cos4 + (ps * qgs_ref[...]) * sin4
        pr2 = pr * pr
        for hd in (2 * pair, 2 * pair + 1):
            nope = qz[:, hd * QK_NOPE_DIM:(hd + 1) * QK_NOPE_DIM]
            own = (lane < QK_ROPE_DIM) if hd % 2 == 0 else (lane >= QK_ROPE_DIM)
            ss = jnp.sum(nope * nope + jnp.where(own, pr2, 0.0), axis=-1, keepdims=True)
            r = lax.rsqrt(ss * (1.0 / QK_HEAD_DIM) + RMS_EPS) * Q_SCALE
            q_ref[hd, :, 0:QK_NOPE_DIM] = ((nope * r) * qgn_ref[...]).astype(BF16)
            rope = rot * r
            if hd % 2 == 1:
                rope = pltpu.roll(rope, QK_ROPE_DIM, 1)
            q_ref[hd, :, QK_NOPE_DIM:] = jnp.where(lane < QK_ROPE_DIM, rope, 0.0).astype(BF16)


def _mix_in(x2d, mod_l, ng, wp, cos, sin):
    t, d = x2d.shape
    tm = min(MIX_TM, t)
    hh = MLA_HEADS
    const2 = lambda i: (0, 0)
    full = lambda a: pl.BlockSpec(a.shape, const2)
    ins = [x2d, mod_l, ng, wp["w_in"], wp["w_kr"], wp["cq_g"], wp["ckv_g"], wp["w_uq"], wp["w_uv"], wp["w_uk"],
           wp["qg_n"], wp["qg_r"], wp["qg_s"], wp["kg_n"], wp["kg_r"], wp["kg_s"]]
    in_specs = [pl.BlockSpec((tm, d), lambda i: (i, 0))] + [full(a) for a in ins[1:]]
    ins += [cos, sin]
    in_specs += [pl.BlockSpec((QK_ROPE_DIM // 2, tm), lambda i: (0, i))] * 2
    return pl.pallas_call(
        _mix_in_kernel,
        grid=(t // tm,),
        in_specs=in_specs,
        out_specs=[
            pl.BlockSpec((hh, tm, QK_PAD), lambda i: (0, i, 0)),
            pl.BlockSpec((hh, QK_PAD, tm), lambda i: (0, 0, i)),
            pl.BlockSpec((hh, tm, V_PAD), lambda i: (0, i, 0)),
            pl.BlockSpec((tm, FNET_WIDTH), lambda i: (i, 0)),
        ],
        out_shape=[
            jax.ShapeDtypeStruct((hh, t, QK_PAD), BF16),
            jax.ShapeDtypeStruct((hh, QK_PAD, t), BF16),
            jax.ShapeDtypeStruct((hh, t, V_PAD), BF16),
            jax.ShapeDtypeStruct((t, FNET_WIDTH), F32),
        ],
        compiler_params=_cparams(("arbitrary",)),
        name="mix_in",
    )(*ins)


ATT_FAST_BOUND = 48.0


def _score_bound(q_gain, k_gain):
    return (Q_SCALE * QK_HEAD_DIM) * jnp.max(jnp.abs(q_gain)) * jnp.max(jnp.abs(k_gain))


def _attn_kernel(fast_ref, q_ref, kt_ref, v_ref, *rest, tc, unroll, n_side):
    side_in = rest[:n_side]
    o_ref = rest[n_side]
    side_out = rest[n_side + 1:2 * n_side + 1]
    acc_ref, m_ref = rest[2 * n_side + 1:]
    n_chunks = kt_ref.shape[2] // tc
    acc_ref[...] = jnp.zeros(acc_ref.shape, F32)

    for wi_ref, wo_ref in zip(side_in, side_out):
        wo_ref[...] = wi_ref[...].astype(BF16)

    def chunk(c):
        off = pl.multiple_of(c * tc, tc)
        return kt_ref[0, :, pl.ds(off, tc)], v_ref[0, pl.ds(off, tc), :]

    @pl.when(fast_ref[0] == 1)
    def _():
        def body(c2, carry):
            for u in range(unroll):
                k_c, v_c = chunk(unroll * c2 + u)
                p = jnp.exp2(_dot(q_ref[0], k_c)).astype(BF16)
                acc_ref[...] += _dot(p, v_c)
            return carry

        lax.fori_loop(0, n_chunks // unroll, body, 0)

    @pl.when(fast_ref[0] != 1)
    def _():
        m_ref[...] = jnp.full(m_ref.shape, -jnp.inf, F32)

        def body(c, carry):
            k_c, v_c = chunk(c)
            s = _dot(q_ref[0], k_c)
            m_prev = m_ref[...]
            m_new = jnp.maximum(m_prev, jnp.max(s, axis=-1, keepdims=True))
            p = jnp.exp2(s - m_new).astype(BF16)
            acc_ref[...] = jnp.exp2(m_prev - m_new) * acc_ref[...] + _dot(p, v_c)
            m_ref[...] = m_new
            return carry

        lax.fori_loop(0, n_chunks, body, 0)

    o_ref[...] = (acc_ref[:, 0:V_HEAD_DIM] * (1.0 / acc_ref[:, V_HEAD_DIM:])).astype(o_ref.dtype)


def _attention(fast, q, kt, v, side=()):
    hh, t, _ = q.shape
    tq = min(ATT_TQ, t)
    tc = min(ATT_TC, t)
    unroll = min(ATT_UNROLL, t // tc)
    once = pl.Buffered(1)
    side_specs = [pl.BlockSpec(blk, (lambda h, i, f, im=im: im(h, i))) for _, blk, im in side]
    outs = pl.pallas_call(
        functools.partial(_attn_kernel, tc=tc, unroll=unroll, n_side=len(side)),
        grid_spec=pltpu.PrefetchScalarGridSpec(
            num_scalar_prefetch=1,
            grid=(hh, t // tq),
            in_specs=[
                pl.BlockSpec((1, tq, QK_PAD), lambda h, i, f: (h, i, 0)),
                pl.BlockSpec((1, QK_PAD, t), lambda h, i, f: (h, 0, 0), pipeline_mode=once),
                pl.BlockSpec((1, t, V_PAD), lambda h, i, f: (h, 0, 0), pipeline_mode=once),
            ] + side_specs,
            out_specs=[pl.BlockSpec((tq, V_HEAD_DIM), lambda h, i, f: (i, h))] + side_specs,
            scratch_shapes=[pltpu.VMEM((tq, V_PAD), F32), pltpu.VMEM((tq, 1), F32)],
        ),
        out_shape=[jax.ShapeDtypeStruct((t, hh * V_HEAD_DIM), BF16)]
        + [jax.ShapeDtypeStruct(w.shape, BF16) for w, _, _ in side],
        compiler_params=_cparams(("arbitrary", "arbitrary"), vmem=ATT_VMEM_LIMIT_BYTES),
        name="attention",
    )(fast, q, kt, v, *[w for w, _, _ in side])
    return outs


def _fnet1_kernel(u_ref, f1_ref, twc_ref, tws_ref, zr_ref, zi_ref):
    k1, n2, width = zr_ref.shape
    y = _dot(f1_ref[...], u_ref[...].astype(BF16))
    a = y[0:n2]
    b = y[n2:]
    for i in range(k1):
        ai = a[:, i * width:(i + 1) * width]
        bi = b[:, i * width:(i + 1) * width]
        c = twc_ref[i]
        s = tws_ref[i]
        zr_ref[i] = (ai * c - bi * s).astype(zr_ref.dtype)
        zi_ref[i] = (-(bi * c) - ai * s).astype(zi_ref.dtype)


def _fnet2_kernel(zr_ref, zi_ref, f2_ref, cs_ref, wf_ref, o_ref):
    n1, k2, _ = zr_ref.shape
    gd = FNET_GROUP_DIM
    xs = []
    for j in range(k2):
        zz = jnp.concatenate([zr_ref[:, j, :], zi_ref[:, j, :]], axis=0).astype(BF16)
        xs.append(_dot(f2_ref[...], zz))
    for g in range(FNET_GROUPS):
        rows = []
        for j in range(k2):
            c0 = g * gd
            rows.append(jnp.concatenate([xs[j][0:n1, c0:c0 + gd], xs[j][n1:, c0:c0 + gd]], axis=1))
        lhs = jnp.concatenate(rows, axis=0).astype(BF16)
        fg = _dot(lhs, cs_ref[...]).astype(BF16)
        og = _dot(fg, wf_ref[g])
        for j in range(k2):
            c0 = j * FNET_WIDTH + g * gd
            o_ref[:, c0:c0 + gd] = og[j * n1:(j + 1) * n1].astype(o_ref.dtype)


def _dft_consts(n1, n2):
    n = n1 * n2
    a2 = 2.0 * np.pi * np.outer(np.arange(n2), np.arange(n2)) / n2
    f1 = np.concatenate([np.cos(a2), np.sin(a2)], axis=0) / n2
    atw = 2.0 * np.pi * np.outer(np.arange(n1), np.arange(n2)) / n
    a1 = 2.0 * np.pi * np.outer(np.arange(n1), np.arange(n1)) / n1
    c1, s1 = np.cos(a1), np.sin(a1)
    f2 = np.block([[c1, s1], [-s1, c1]])
    ac = 2.0 * np.pi * np.outer(np.arange(FNET_GROUP_DIM), np.arange(FNET_GROUP_DIM)) / FNET_GROUP_DIM
    norm = n2 / math.sqrt(n * FNET_GROUP_DIM)
    cs = np.concatenate([np.cos(ac), np.sin(ac)], axis=0) * norm
    return (jnp.asarray(f1, BF16), jnp.asarray(np.cos(atw)[:, :, None], F32), jnp.asarray(np.sin(atw)[:, :, None], F32),
            jnp.asarray(f2, BF16), jnp.asarray(cs, BF16))


def _fnet(u, wf_b, n1, n2):
    t, width = u.shape
    assert t == n1 * n2
    f1, twc, tws, f2, cs = _dft_consts(n1, n2)
    k1 = min(FNET_K1, n1)
    k2 = min(FNET_K2, n2)
    const2 = lambda i: (0, 0)
    zr, zi = pl.pallas_call(
        _fnet1_kernel,
        grid=(n1 // k1,),
        in_specs=[
            pl.BlockSpec((n2, k1 * width), lambda i: (0, i)),
            pl.BlockSpec(f1.shape, const2),
            pl.BlockSpec((k1, n2, 1), lambda i: (i, 0, 0)),
            pl.BlockSpec((k1, n2, 1), lambda i: (i, 0, 0)),
        ],
        out_specs=[pl.BlockSpec((k1, n2, width), lambda i: (i, 0, 0))] * 2,
        out_shape=[jax.ShapeDtypeStruct((n1, n2, width), F32)] * 2,
        compiler_params=_cparams(("arbitrary",)),
        name="fnet_stage1",
    )(u.reshape(n2, n1 * width), f1, twc, tws)
    out = pl.pallas_call(
        _fnet2_kernel,
        grid=(n2 // k2,),
        in_specs=[
            pl.BlockSpec((n1, k2, width), lambda i: (0, i, 0)),
            pl.BlockSpec((n1, k2, width), lambda i: (0, i, 0)),
            pl.BlockSpec(f2.shape, const2),
            pl.BlockSpec(cs.shape, const2),
            pl.BlockSpec(wf_b.shape, lambda i: (0, 0, 0)),
        ],
        out_specs=pl.BlockSpec((n1, k2 * width), lambda i: (0, i)),
        out_shape=jax.ShapeDtypeStruct((n1, n2 * width), BF16),
        compiler_params=_cparams(("arbitrary",)),
        name="fnet_stage2",
    )(zr, zi, f2, cs, wf_b)
    return out.reshape(t, width)


def _mix_out_kernel(a_ref, f_ref, x_ref, mod_ref, og_ref, w_out_ref, fg_ref, *rest, moe):
    d = x_ref.shape[1]
    half = a_ref.shape[1]
    an = _rms(a_ref[...].astype(F32), og_ref[:, 0:half])
    fn = _rms(f_ref[...].astype(F32), og_ref[:, half:])
    cat = jnp.concatenate([an, fn], axis=1).astype(BF16)
    o = _dot(cat, w_out_ref[...])
    gate1 = mod_ref[:, 2 * d:3 * d]
    shift2 = mod_ref[:, 3 * d:4 * d]
    scale2 = mod_ref[:, 4 * d:5 * d]
    x1 = x_ref[...] + gate1 * o
    h2 = _rms(x1, fg_ref[...]) * (1.0 + scale2) + shift2
    if not moe:
        x1_ref, h2_ref = rest
        x1_ref[...] = x1
        h2_ref[...] = h2.astype(h2_ref.dtype)
        return
    wr_ref, tri_ref, x1_ref, h2_ref, route_ref, cnt_ref, carry_ref = rest
    x1_ref[...] = x1
    h2_ref[...] = h2

    @pl.when(pl.program_id(0) == 0)
    def _():
        carry_ref[...] = jnp.zeros(carry_ref.shape, F32)

    tm = x1.shape[0]
    h_hi = h2.astype(BF16)
    h_lo = (h2 - h_hi.astype(F32)).astype(BF16)
    hw = _dot(h_hi, wr_ref[...])
    logits = (hw[:, 0:LANES] + hw[:, LANES:]) + _dot(h_lo, wr_ref[:, 0:LANES])
    lane = lax.broadcasted_iota(jnp.int32, (tm, LANES), 1)
    neg = jnp.float32(-jnp.inf)
    logits = jnp.where(lane < N_EXPERTS, logits, neg)
    m1 = jnp.max(logits, axis=-1, keepdims=True)
    i1 = jnp.min(jnp.where(logits == m1, lane, LANES), axis=-1, keepdims=True)
    rest_l = jnp.where(lane == i1, neg, logits)
    m2 = jnp.max(rest_l, axis=-1, keepdims=True)
    i2 = jnp.min(jnp.where(rest_l == m2, lane, LANES), axis=-1, keepdims=True)
    e21 = jnp.exp(m2 - m1)
    w1 = 1.0 / (1.0 + e21)
    w2 = e21 / (1.0 + e21)
    oh1 = jnp.where(lane == i1, 1.0, 0.0)
    oh2 = jnp.where(lane == i2, 1.0, 0.0)
    pre = _dot(tri_ref[...], jnp.concatenate([oh1, oh2], axis=1).astype(BF16))
    cnt1 = jnp.sum(oh1, axis=0, keepdims=True)
    cnt2 = jnp.sum(oh2, axis=0, keepdims=True)
    carry = carry_ref[0:1, :]
    rank1 = jnp.sum(oh1 * (carry + pre[:, 0:LANES]), axis=-1, keepdims=True)
    rank2 = jnp.sum(oh2 * (carry + cnt1 + pre[:, LANES:]), axis=-1, keepdims=True)
    new_carry = carry + cnt1 + cnt2
    carry_ref[...] = jnp.broadcast_to(new_carry, carry_ref.shape)
    cnt_ref[...] = jnp.broadcast_to(new_carry, cnt_ref.shape)
    vals = (i1.astype(F32), i2.astype(F32), w1, w2, rank1, rank2)
    route = jnp.zeros((tm, LANES), F32)
    for k, val in enumerate(vals):
        route = jnp.where(lane == k, val, route)
    route_ref[...] = route


def _mix_out(a, f, x2d, mod_l, og, w_out_b, fg, router=None):
    t, d = x2d.shape
    tm = min(MIX_TM, t)
    half = a.shape[1]
    moe = router is not None
    const2 = lambda i: (0, 0)
    row = lambda w: pl.BlockSpec((tm, w), lambda i: (i, 0))
    ins = [a, f, x2d, mod_l, og, w_out_b, fg]
    in_specs = [row(half), row(half), row(d), pl.BlockSpec(mod_l.shape, const2), pl.BlockSpec(og.shape, const2),
                pl.BlockSpec(w_out_b.shape, const2), pl.BlockSpec(fg.shape, const2)]
    out_specs = [row(d), row(d)]
    out_shape = [jax.ShapeDtypeStruct((t, d), F32), jax.ShapeDtypeStruct((t, d), F32 if moe else BF16)]
    scratch = []
    if moe:
        tri = jnp.asarray(np.tril(np.ones((tm, tm), np.float32), -1), BF16)
        ins += [router, tri]
        in_specs += [pl.BlockSpec(router.shape, const2), pl.BlockSpec(tri.shape, const2)]
        out_specs += [row(LANES), pl.BlockSpec((8, LANES), const2)]
        out_shape += [jax.ShapeDtypeStruct((t, LANES), F32), jax.ShapeDtypeStruct((8, LANES), F32)]
        scratch = [pltpu.VMEM((8, LANES), F32)]
    return pl.pallas_call(
        functools.partial(_mix_out_kernel, moe=moe),
        grid=(t // tm,),
        in_specs=in_specs,
        out_specs=out_specs,
        out_shape=out_shape,
        scratch_shapes=scratch,
        compiler_params=_cparams(("arbitrary",)),
        name="mix_out_moe" if moe else "mix_out",
    )(*ins)


def _silu(g):
    return g * (1.0 / (1.0 + jnp.exp(-g)))


def _ffn_kernel(h_ref, x1_ref, mod_ref, wg_ref, wu_ref, wd_ref, o_ref):
    j = pl.program_id(1)
    d = x1_ref.shape[1]

    @pl.when(j == 0)
    def _():
        o_ref[...] = jnp.zeros(o_ref.shape, F32)

    h = h_ref[...]
    act = (_silu(_dot(h, wg_ref[...])) * _dot(h, wu_ref[...])).astype(BF16)
    o_ref[...] += _dot(act, wd_ref[...])

    @pl.when(j == pl.num_programs(1) - 1)
    def _():
        o_ref[...] = x1_ref[...] + mod_ref[:, 5 * d:6 * d] * o_ref[...]


def _ffn_dense(h2, x1, mod_l, wg, wu, wd):
    t, d = x1.shape
    ff = wg.shape[1]
    tm = min(FFN_TM, t)
    tf = min(FFN_TF, ff)
    return pl.pallas_call(
        _ffn_kernel,
        grid=(t // tm, ff // tf),
        in_specs=[
            pl.BlockSpec((tm, d), lambda i, j: (i, 0)),
            pl.BlockSpec((tm, d), lambda i, j: (i, 0)),
            pl.BlockSpec(mod_l.shape, lambda i, j: (0, 0)),
            pl.BlockSpec((d, tf), lambda i, j: (0, j)),
            pl.BlockSpec((d, tf), lambda i, j: (0, j)),
            pl.BlockSpec((tf, d), lambda i, j: (j, 0)),
        ],
        out_specs=pl.BlockSpec((tm, d), lambda i, j: (i, 0)),
        out_shape=jax.ShapeDtypeStruct((t, d), F32),
        compiler_params=_cparams(("arbitrary", "arbitrary")),
        name="ffn_dense",
    )(h2, x1, mod_l, wg, wu, wd)


def _dispatch_kernel(d1_ref, d2_ref, tail_ref, h_ref, xs_ref, zero_ref, sem, zsem):
    tm = h_ref.shape[0]
    base = pl.program_id(0) * tm
    blk = zero_ref.shape[0]

    @pl.when(pl.program_id(0) == 0)
    def _():
        zero_ref[...] = jnp.zeros(zero_ref.shape, zero_ref.dtype)

        def fill(e):
            return pltpu.make_async_copy(zero_ref, xs_ref.at[pl.ds(pl.multiple_of(tail_ref[e], blk), blk), :], zsem)

        for e in range(tail_ref.shape[0]):
            @pl.when(tail_ref[e] >= 0)
            def _():
                fill(e).start()

        for e in range(tail_ref.shape[0]):
            @pl.when(tail_ref[e] >= 0)
            def _():
                fill(e).wait()

    def copy(r, dst):
        return pltpu.make_async_copy(h_ref.at[pl.ds(r, 1), :], xs_ref.at[pl.ds(dst, 1), :], sem)

    def start(r, c):
        copy(r, d1_ref[base + r]).start()
        copy(r, d2_ref[base + r]).start()
        return c

    def wait(r, c):
        copy(r, d1_ref[base + r]).wait()
        copy(r, d2_ref[base + r]).wait()
        return c

    lax.fori_loop(0, tm, start, 0, unroll=ROW_DMA_UNROLL)
    lax.fori_loop(0, tm, wait, 0, unroll=ROW_DMA_UNROLL)


def _dispatch(dest1, dest2, tail, h2, n_rows, blk):
    t, d = h2.shape
    tm = min(ROW_TM, t)
    return pl.pallas_call(
        _dispatch_kernel,
        grid_spec=pltpu.PrefetchScalarGridSpec(
            num_scalar_prefetch=3,
            grid=(t // tm,),
            in_specs=[pl.BlockSpec((tm, d), lambda i, d1, d2, tl: (i, 0))],
            out_specs=pl.BlockSpec(memory_space=pl.ANY),
            scratch_shapes=[pltpu.VMEM((blk, d), h2.dtype), pltpu.SemaphoreType.DMA(()), pltpu.SemaphoreType.DMA(())],
        ),
        out_shape=jax.ShapeDtypeStruct((n_rows, d), h2.dtype),
        compiler_params=_cparams(("arbitrary",)),
        name="moe_dispatch",
    )(dest1, dest2, tail, h2)


def _expert_kernel(be_ref, nu_ref, x_ref, wg_ref, wu_ref, wd_ref, o_ref, xb_ref):
    b = pl.program_id(0)
    j = pl.program_id(1)

    @pl.when(j == 0)
    def _():
        o_ref[...] = jnp.zeros(o_ref.shape, o_ref.dtype)
        xb_ref[...] = x_ref[...].astype(BF16)

    @pl.when(b < nu_ref[0])
    def _():
        xb = xb_ref[...]
        act = (_silu(_dot(xb, wg_ref[0])) * _dot(xb, wu_ref[0])).astype(BF16)
        o_ref[...] += _dot(act, wd_ref[0])


def _experts(block_e, n_used, xs, wg, wu, wd):
    n_rows, d = xs.shape
    ne, _, ff = wg.shape
    blk = min(MOE_BLK, n_rows)
    tf = min(MOE_TF, ff)
    nb = n_rows // blk
    nj = ff // tf

    def last_live(b, nu):
        return jnp.maximum(jnp.minimum(b, nu[0] - 1), 0)

    def row_map(b, j, be, nu):
        return (last_live(b, nu), 0)

    def col_map(b, j, be, nu):
        return (be[last_live(b, nu)], 0, jnp.where(b < nu[0], j, nj - 1))

    def down_map(b, j, be, nu):
        return (be[last_live(b, nu)], jnp.where(b < nu[0], j, nj - 1), 0)

    return pl.pallas_call(
        _expert_kernel,
        grid_spec=pltpu.PrefetchScalarGridSpec(
            num_scalar_prefetch=2,
            grid=(nb, nj),
            in_specs=[
                pl.BlockSpec((blk, d), row_map),
                pl.BlockSpec((1, d, tf), col_map),
                pl.BlockSpec((1, d, tf), col_map),
                pl.BlockSpec((1, tf, d), down_map),
            ],
            out_specs=pl.BlockSpec((blk, d), lambda b, j, be, nu: (b, 0)),
            scratch_shapes=[pltpu.VMEM((blk, d), BF16)],
        ),
        out_shape=jax.ShapeDtypeStruct((n_rows, d), F32),
        compiler_params=_cparams(("arbitrary", "arbitrary")),
        name="moe_experts",
    )(block_e, n_used, xs, wg, wu, wd)


def _combine_kernel(d1_ref, d2_ref, yb_ref, route_ref, x1_ref, mod_ref, o_ref, buf_ref, sem):
    tm, d = x1_ref.shape
    base = pl.program_id(0) * tm

    def copy(slot, r, src):
        return pltpu.make_async_copy(yb_ref.at[pl.ds(src, 1), :], buf_ref.at[slot, pl.ds(r, 1), :], sem)

    def start(r, c):
        copy(0, r, d1_ref[base + r]).start()
        copy(1, r, d2_ref[base + r]).start()
        return c

    def wait(r, c):
        copy(0, r, d1_ref[base + r]).wait()
        copy(1, r, d2_ref[base + r]).wait()
        return c

    lax.fori_loop(0, tm, start, 0, unroll=ROW_DMA_UNROLL)
    lax.fori_loop(0, tm, wait, 0, unroll=ROW_DMA_UNROLL)
    w1 = route_ref[:, 2:3]
    w2 = route_ref[:, 3:4]
    y = buf_ref[0] * w1 + buf_ref[1] * w2
    o_ref[...] = x1_ref[...] + mod_ref[:, 5 * d:6 * d] * y


def _combine(dest1, dest2, yb, route, x1, mod_l):
    t, d = x1.shape
    tm = min(ROW_TM, t)
    row = lambda w: pl.BlockSpec((tm, w), lambda i, d1, d2: (i, 0))
    return pl.pallas_call(
        _combine_kernel,
        grid_spec=pltpu.PrefetchScalarGridSpec(
            num_scalar_prefetch=2,
            grid=(t // tm,),
            in_specs=[pl.BlockSpec(memory_space=pl.ANY), row(LANES), row(d),
                      pl.BlockSpec(mod_l.shape, lambda i, d1, d2: (0, 0))],
            out_specs=row(d),
            scratch_shapes=[pltpu.VMEM((2, tm, d), F32), pltpu.SemaphoreType.DMA(())],
        ),
        out_shape=jax.ShapeDtypeStruct((t, d), F32),
        compiler_params=_cparams(("arbitrary",)),
        name="moe_combine",
    )(dest1, dest2, yb, route, x1, mod_l)


def _moe_plan(route, cnt, t, blk):
    e1 = route[:, 0].astype(jnp.int32)
    e2 = route[:, 1].astype(jnp.int32)
    r1 = route[:, 4].astype(jnp.int32)
    r2 = route[:, 5].astype(jnp.int32)
    counts = cnt[0, :N_EXPERTS].astype(jnp.int32)
    padded = (counts + blk - 1) // blk * blk
    pend = jnp.cumsum(padded)
    pstart = pend - padded
    eids = jnp.arange(N_EXPERTS, dtype=jnp.int32)
    dest1 = jnp.sum(jnp.where(e1[:, None] == eids[None, :], pstart[None, :], 0), axis=1) + r1
    dest2 = jnp.sum(jnp.where(e2[:, None] == eids[None, :], pstart[None, :], 0), axis=1) + r2
    n_blocks = -(-(2 * t) // blk) + N_EXPERTS
    bstart = jnp.arange(n_blocks, dtype=jnp.int32) * blk
    block_e = jnp.minimum(jnp.sum((pend[None, :] <= bstart[:, None]).astype(jnp.int32), axis=1), N_EXPERTS - 1)
    n_used = (pend[-1] // blk).astype(jnp.int32).reshape(1)
    tail = jnp.where(padded > 0, pend - blk, -1)
    spare = bstart[(2 * t) // blk:]
    tail = jnp.concatenate([tail, jnp.where(spare >= pend[-1], spare, -1)]).astype(jnp.int32)
    return (dest1.astype(jnp.int32), dest2.astype(jnp.int32), block_e.astype(jnp.int32), n_used, tail,
            n_blocks * blk)


def _prep_mixer_weights(w_in, cq_norm, ckv_norm, w_uq, w_ukv, q_norm, k_norm):
    s1 = Q_LORA_RANK
    s2 = s1 + KV_LORA_RANK
    s3 = s2 + QK_ROPE_DIM
    half = QK_ROPE_DIM // 2
    swap = np.concatenate([np.arange(half, QK_ROPE_DIM), np.arange(half)])
    w_kr = w_in[:, s2:s3]
    uq = w_uq.reshape(Q_LORA_RANK, MLA_HEADS, QK_HEAD_DIM)
    uq_rope = uq[:, :, QK_NOPE_DIM:]
    ukv = w_ukv.reshape(KV_LORA_RANK, MLA_HEADS, QK_NOPE_DIM + V_HEAD_DIM)
    qg_r = q_norm[QK_NOPE_DIM:]
    kg_r = k_norm[QK_NOPE_DIM:]
    reps = LANES // QK_ROPE_DIM
    return {
        "w_in": jnp.concatenate([w_in[:, :s2], w_in[:, s3:]], axis=1).astype(BF16),
        "w_kr": jnp.concatenate([w_kr, w_kr[:, swap]], axis=1).T.astype(BF16),
        "cq_g": cq_norm.reshape(1, -1),
        "ckv_g": ckv_norm.reshape(1, -1),
        "w_uq": jnp.concatenate([uq[:, :, :QK_NOPE_DIM].reshape(Q_LORA_RANK, -1),
                                 uq_rope.reshape(Q_LORA_RANK, -1),
                                 uq_rope[:, :, swap].reshape(Q_LORA_RANK, -1)], axis=1).astype(BF16),
        "w_uv": ukv[:, :, QK_NOPE_DIM:].reshape(KV_LORA_RANK, -1).astype(BF16),
        "w_uk": ukv[:, :, :QK_NOPE_DIM].reshape(KV_LORA_RANK, -1).T.astype(BF16),
        "qg_n": q_norm[:QK_NOPE_DIM].reshape(1, -1),
        "qg_r": jnp.tile(qg_r, reps).reshape(1, -1),
        "qg_s": jnp.tile(qg_r[swap], reps).reshape(1, -1),
        "kg_n": k_norm[:QK_NOPE_DIM].reshape(-1, 1),
        "kg_r": kg_r.reshape(-1, 1),
        "kg_s": kg_r[swap].reshape(-1, 1),
    }


BF16_ROWS = 16


def _row_blocked(w, n_qt):
    steps = MLA_HEADS * n_qt
    tiles = w.shape[0] // BF16_ROWS
    n_blk = max(k for k in range(1, steps + 1) if tiles % k == 0)
    return w, (w.shape[0] // n_blk, w.shape[1]), lambda h, i: (jnp.minimum(h * n_qt + i, n_blk - 1), 0)


def _expert_blocked(w, n_qt):
    assert w.shape[0] == MLA_HEADS and w.shape[1] % (n_qt * BF16_ROWS) == 0
    return w, (1, w.shape[1] // n_qt, w.shape[2]), lambda h, i: (h, i, 0)


def _fnet_factors(t):
    n2 = 1 << (int(math.log2(t)) // 2)
    return t // n2, n2


def kernel(x, c, positions, ada_w, ada_b, mix_norm, w_in, cq_norm, ckv_norm, w_uq, w_ukv, q_norm, k_norm, w_fnet,
           out_norm, w_out, ffn_norm, dense_w_gate, dense_w_up, dense_w_down, router_w, moe_w_gate, moe_w_up,
           moe_w_down):
    b, s, d = x.shape
    assert b == 1
    t = b * s
    depth = ada_w.shape[0]
    assert depth == 2 and dense_w_gate.shape[0] == 1 and moe_w_gate.shape[0] == 1
    x2d = x.reshape(t, d)
    mod = _ada_mod(c.reshape(d, 1), ada_w, ada_b.reshape(depth, 1, -1))
    cos, sin = _rope_tables(positions.reshape(t).astype(F32))
    n1, n2 = _fnet_factors(t)
    n_qt = t // min(ATT_TQ, t)
    side_jobs = [
        [_row_blocked(dense_w_gate[0], n_qt), _row_blocked(dense_w_up[0], n_qt), _row_blocked(dense_w_down[0], n_qt),
         _expert_blocked(moe_w_gate[0], n_qt)],
        [_expert_blocked(moe_w_up[0], n_qt), _expert_blocked(moe_w_down[0], n_qt)],
    ]
    cast = []
    for l in range(depth):
        mod_l = mod[l]
        wp = _prep_mixer_weights(w_in[l], cq_norm[l], ckv_norm[l], w_uq[l], w_ukv[l], q_norm[l], k_norm[l])
        fast = (_score_bound(q_norm[l], k_norm[l]) <= ATT_FAST_BOUND).astype(jnp.int32).reshape(1)
        q, kt, v, u = _mix_in(x2d, mod_l, mix_norm[l].reshape(1, d), wp, cos, sin)
        a, *new_cast = _attention(fast, q, kt, v, side_jobs[l])
        cast += new_cast
        f = _fnet(u, w_fnet[l].astype(BF16), n1, n2)
        og = out_norm[l].reshape(1, d)
        fg = ffn_norm[l].reshape(1, d)
        w_out_b = w_out[l].astype(BF16)
        j = l // 2
        if l % 2 == 0:
            x1, h2 = _mix_out(a, f, x2d, mod_l, og, w_out_b, fg)
            x2d = _ffn_dense(h2, x1, mod_l, cast[0], cast[1], cast[2])
        else:
            wr = jnp.pad(router_w[j], ((0, 0), (0, LANES - N_EXPERTS)))
            wr_hi = wr.astype(BF16)
            router = jnp.concatenate([wr_hi, (wr - wr_hi.astype(F32)).astype(BF16)], axis=1)
            x1, h2, route, cnt = _mix_out(a, f, x2d, mod_l, og, w_out_b, fg, router=router)
            dest1, dest2, block_e, n_used, tail, n_rows = _moe_plan(route, cnt, t, MOE_BLK)
            xs = _dispatch(dest1, dest2, tail, h2, n_rows, MOE_BLK)
            yb = _experts(block_e, n_used, xs, cast[3], cast[4], cast[5])
            x2d = _combine(dest1, dest2, yb, route, x1, mod_l)
    return x2d.reshape(b, s, d)
```

```python
import functools
import math

import numpy as np
import jax
import jax.numpy as jnp
from jax import lax
from jax.experimental import pallas as pl
from jax.experimental.pallas import tpu as pltpu

D_MODEL = 2048
DEPTH = 2
MLA_HEADS = 8
QK_NOPE_DIM = 128
QK_ROPE_DIM = 64
V_HEAD_DIM = 128
QK_HEAD_DIM = QK_NOPE_DIM + QK_ROPE_DIM
Q_LORA_RANK = 512
KV_LORA_RANK = 256
MLA_WIDTH = MLA_HEADS * V_HEAD_DIM
FNET_GROUPS = 8
FNET_GROUP_DIM = 128
FNET_WIDTH = FNET_GROUPS * FNET_GROUP_DIM
ROPE_THETA = 10000.0
N_EXPERTS = 8
N_MOD = 6
RMS_EPS = 1e-6

LANES = 128
QK_PAD = 256
V_PAD = 256
VMEM_LIMIT_BYTES = 56 * 1024 * 1024
ATT_VMEM_LIMIT_BYTES = 60 * 1024 * 1024

ADA_TN = 1024
ROPE_TN = 2048
MIX_TM = 512
ATT_TQ = 1024
ATT_TC = 1024
ATT_UNROLL = 8
FNET_K1 = 4
FNET_K2 = 8
FFN_TM = 512
FFN_TF = 512
MOE_BLK = 512
MOE_TF = 1024
ROW_TM = 256
ROW_DMA_UNROLL = 8

BF16 = jnp.bfloat16
F32 = jnp.float32
NT_DIMS = (((1,), (1,)), ((), ()))


def _cparams(sem, vmem=VMEM_LIMIT_BYTES):
    return pltpu.CompilerParams(dimension_semantics=sem, vmem_limit_bytes=vmem)


def _dot(a, b):
    return jnp.dot(a, b, preferred_element_type=F32)


def _rms(x, g):
    ms = jnp.mean(x * x, axis=-1, keepdims=True)
    return (x * lax.rsqrt(ms + RMS_EPS)) * g


def _ada_kernel(c_ref, w_ref, b_ref, o_ref, cond_ref):
    d = w_ref.shape[1]
    tn = w_ref.shape[2]
    rows = 64

    @pl.when(jnp.logical_and(pl.program_id(0) == 0, pl.program_id(1) == 0))
    def _():
        c = c_ref[...]
        cond_ref[...] = jnp.broadcast_to(c * (1.0 / (1.0 + jnp.exp(-c))), cond_ref.shape)

    def body(i, acc):
        r = pl.multiple_of(i * rows, rows)
        cond = cond_ref[pl.ds(r, rows), :]
        for s in range(rows // 8):
            cs = cond[8 * s:8 * s + 8]
            w8 = w_ref[0, pl.ds(r + 8 * s, 8), :]
            acc = acc + w8 * jnp.concatenate([cs] * (tn // LANES), axis=1)
        return acc

    acc = lax.fori_loop(0, d // rows, body, jnp.zeros((8, tn), F32))
    o_ref[0] = jnp.sum(acc, axis=0, keepdims=True) + b_ref[0]


def _ada_mod(c_col, ada_w, ada_b3):
    depth, d, n = ada_w.shape
    tn = min(ADA_TN, n)
    return pl.pallas_call(
        _ada_kernel,
        grid=(depth, n // tn),
        in_specs=[
            pl.BlockSpec((d, 1), lambda l, j: (0, 0)),
            pl.BlockSpec((1, d, tn), lambda l, j: (l, 0, j)),
            pl.BlockSpec((1, 1, tn), lambda l, j: (l, 0, j)),
        ],
        out_specs=pl.BlockSpec((1, 1, tn), lambda l, j: (l, 0, j)),
        out_shape=jax.ShapeDtypeStruct((depth, 1, n), F32),
        scratch_shapes=[pltpu.VMEM((d, LANES), F32)],
        compiler_params=_cparams(("arbitrary", "arbitrary")),
        name="ada_mod",
    )(c_col, ada_w, ada_b3)


def _rope_kernel(pos_ref, invf_ref, cos_ref, sin_ref):
    ang = invf_ref[...] * pos_ref[...]
    cos_ref[...] = jnp.cos(ang)
    sin_ref[...] = jnp.sin(ang)


def _rope_tables(pos_f32):
    t = pos_f32.shape[0]
    tn = min(ROPE_TN, t)
    half = QK_ROPE_DIM // 2
    inv_freq = ROPE_THETA ** (-jnp.arange(half, dtype=F32) / half)
    return pl.pallas_call(
        _rope_kernel,
        grid=(t // tn,),
        in_specs=[pl.BlockSpec((1, tn), lambda i: (0, i)), pl.BlockSpec((half, 1), lambda i: (0, 0))],
        out_specs=[pl.BlockSpec((half, tn), lambda i: (0, i))] * 2,
        out_shape=[jax.ShapeDtypeStruct((half, t), F32)] * 2,
        compiler_params=_cparams(("arbitrary",)),
        name="rope_tables",
    )(pos_f32.reshape(1, t), inv_freq.reshape(half, 1))


Q_SCALE = (QK_HEAD_DIM ** -0.5) * math.log2(math.e)


def _mix_in_kernel(x_ref, mod_ref, ng_ref, w_in_ref, w_kr_ref, cqg_ref, ckvg_ref, w_uq_ref, w_uv_ref, w_uk_ref,
                   qgn_ref, qgr_ref, qgs_ref, kgn_ref, kgr_ref, kgs_ref, cos_ref, sin_ref,
                   q_ref, kt_ref, v_ref, u_ref):
    d = x_ref.shape[1]
    cos = cos_ref[...]
    sin = sin_ref[...]
    cos_t = jnp.concatenate([cos, cos], axis=0)
    sin_t = jnp.concatenate([-sin, sin], axis=0)
    cos4 = jnp.concatenate([cos_t, cos_t], axis=0).T
    sin4 = jnp.concatenate([sin_t, sin_t], axis=0).T
    x = x_ref[...]
    shift = mod_ref[:, 0:d]
    scale = mod_ref[:, d:2 * d]
    h = _rms(x, ng_ref[...]) * (1.0 + scale) + shift
    hb = h.astype(BF16)
    z = _dot(hb, w_in_ref[...])
    c_q = z[:, 0:Q_LORA_RANK]
    c_kv = z[:, Q_LORA_RANK:Q_LORA_RANK + KV_LORA_RANK]
    u_ref[...] = z[:, Q_LORA_RANK + KV_LORA_RANK:]
    cqn = _rms(c_q, cqg_ref[...]).astype(BF16)
    ckvn = _rms(c_kv, ckvg_ref[...]).astype(BF16)

    kn_t = lax.dot_general(w_uk_ref[...], ckvn, NT_DIMS, preferred_element_type=F32)
    kr_t = lax.dot_general(w_kr_ref[...], hb, NT_DIMS, preferred_element_type=F32)
    kr = kr_t[0:QK_ROPE_DIM]
    kr_rot = (kr * kgr_ref[...]) * cos_t + (kr_t[QK_ROPE_DIM:] * kgs_ref[...]) * sin_t
    ss_r = jnp.sum(kr * kr, axis=0, keepdims=True)
    tm = x.shape[0]
    pad_rows = jnp.zeros((QK_PAD - QK_HEAD_DIM, tm), BF16)
    for hd in range(MLA_HEADS):
        kn = kn_t[hd * QK_NOPE_DIM:(hd + 1) * QK_NOPE_DIM]
        ss = jnp.sum(kn * kn, axis=0, keepdims=True) + ss_r
        r = lax.rsqrt(ss * (1.0 / QK_HEAD_DIM) + RMS_EPS)
        kt_ref[hd, 0:QK_NOPE_DIM, :] = ((kn * r) * kgn_ref[...]).astype(BF16)
        kt_ref[hd, QK_NOPE_DIM:QK_HEAD_DIM, :] = (kr_rot * r).astype(BF16)
        kt_ref[hd, QK_HEAD_DIM:, :] = pad_rows

    vz = _dot(ckvn, w_uv_ref[...])
    ones = jnp.ones((tm, V_PAD - V_HEAD_DIM), BF16)
    for hd in range(MLA_HEADS):
        v_ref[hd, :, 0:V_HEAD_DIM] = vz[:, hd * V_HEAD_DIM:(hd + 1) * V_HEAD_DIM].astype(BF16)
        v_ref[hd, :, V_HEAD_DIM:] = ones

    qz = _dot(cqn, w_uq_ref[...])
    n_nope = MLA_HEADS * QK_NOPE_DIM
    n_rope = MLA_HEADS * QK_ROPE_DIM
    lane = lax.broadcasted_iota(jnp.int32, (x.shape[0], LANES), 1)
    for pair in range(MLA_HEADS // 2):
        pr = qz[:, n_nope + pair * LANES:n_nope + (pair + 1) * LANES]
        ps = qz[:, n_nope + n_rope + pair * LANES:n_nope + n_rope + (pair + 1) * LANES]
        rot = (pr * qgr_ref[...]) * cos4 + (ps * qgs_ref[...]) * sin4
        pr2 = pr * pr
        for hd in (2 * pair, 2 * pair + 1):
            nope = qz[:, hd * QK_NOPE_DIM:(hd + 1) * QK_NOPE_DIM]
            own = (lane < QK_ROPE_DIM) if hd % 2 == 0 else (lane >= QK_ROPE_DIM)
            ss = jnp.sum(nope * nope + jnp.where(own, pr2, 0.0), axis=-1, keepdims=True)
            r = lax.rsqrt(ss * (1.0 / QK_HEAD_DIM) + RMS_EPS) * Q_SCALE
            q_ref[hd, :, 0:QK_NOPE_DIM] = ((nope * r) * qgn_ref[...]).astype(BF16)
            rope = rot * r
            if hd % 2 == 1:
                rope = pltpu.roll(rope, QK_ROPE_DIM, 1)
            q_ref[hd, :, QK_NOPE_DIM:] = jnp.where(lane < QK_ROPE_DIM, rope, 0.0).astype(BF16)


def _mix_in(x2d, mod_l, ng, wp, cos, sin):
    t, d = x2d.shape
    tm = min(MIX_TM, t)
    hh = MLA_HEADS
    const2 = lambda i: (0, 0)
    full = lambda a: pl.BlockSpec(a.shape, const2)
    ins = [x2d, mod_l, ng, wp["w_in"], wp["w_kr"], wp["cq_g"], wp["ckv_g"], wp["w_uq"], wp["w_uv"], wp["w_uk"],
           wp["qg_n"], wp["qg_r"], wp["qg_s"], wp["kg_n"], wp["kg_r"], wp["kg_s"]]
    in_specs = [pl.BlockSpec((tm, d), lambda i: (i, 0))] + [full(a) for a in ins[1:]]
    ins += [cos, sin]
    in_specs += [pl.BlockSpec((QK_ROPE_DIM // 2, tm), lambda i: (0, i))] * 2
    return pl.pallas_call(
        _mix_in_kernel,
        grid=(t // tm,),
        in_specs=in_specs,
        out_specs=[
            pl.BlockSpec((hh, tm, QK_PAD), lambda i: (0, i, 0)),
            pl.BlockSpec((hh, QK_PAD, tm), lambda i: (0, 0, i)),
            pl.BlockSpec((hh, tm, V_PAD), lambda i: (0, i, 0)),
            pl.BlockSpec((tm, FNET_WIDTH), lambda i: (i, 0)),
        ],
        out_shape=[
            jax.ShapeDtypeStruct((hh, t, QK_PAD), BF16),
            jax.ShapeDtypeStruct((hh, QK_PAD, t), BF16),
            jax.ShapeDtypeStruct((hh, t, V_PAD), BF16),
            jax.ShapeDtypeStruct((t, FNET_WIDTH), F32),
        ],
        compiler_params=_cparams(("arbitrary",)),
        name="mix_in",
    )(*ins)


ATT_FAST_BOUND = 48.0


def _score_bound(q_gain, k_gain):
    return (Q_SCALE * QK_HEAD_DIM) * jnp.max(jnp.abs(q_gain)) * jnp.max(jnp.abs(k_gain))


def _attn_kernel(fast_ref, q_ref, kt_ref, v_ref, *rest, tc, unroll, n_side):
    side_in = rest[:n_side]
    o_ref = rest[n_side]
    side_out = rest[n_side + 1:2 * n_side + 1]
    acc_ref, m_ref = rest[2 * n_side + 1:]
    n_chunks = kt_ref.shape[2] // tc
    acc_ref[...] = jnp.zeros(acc_ref.shape, F32)

    for wi_ref, wo_ref in zip(side_in, side_out):
        wo_ref[...] = wi_ref[...].astype(BF16)

    def chunk(c):
        off = pl.multiple_of(c * tc, tc)
        return kt_ref[0, :, pl.ds(off, tc)], v_ref[0, pl.ds(off, tc), :]

    @pl.when(fast_ref[0] == 1)
    def _():
        def body(c2, carry):
            for u in range(unroll):
                k_c, v_c = chunk(unroll * c2 + u)
                p = jnp.exp2(_dot(q_ref[0], k_c)).astype(BF16)
                acc_ref[...] += _dot(p, v_c)
            return carry

        lax.fori_loop(0, n_chunks // unroll, body, 0)

    @pl.when(fast_ref[0] != 1)
    def _():
        m_ref[...] = jnp.full(m_ref.shape, -jnp.inf, F32)

        def body(c, carry):
            k_c, v_c = chunk(c)
            s = _dot(q_ref[0], k_c)
            m_prev = m_ref[...]
            m_new = jnp.maximum(m_prev, jnp.max(s, axis=-1, keepdims=True))
            p = jnp.exp2(s - m_new).astype(BF16)
            acc_ref[...] = jnp.exp2(m_prev - m_new) * acc_ref[...] + _dot(p, v_c)
            m_ref[...] = m_new
            return carry

        lax.fori_loop(0, n_chunks, body, 0)

    o_ref[...] = (acc_ref[:, 0:V_HEAD_DIM] * (1.0 / acc_ref[:, V_HEAD_DIM:])).astype(o_ref.dtype)


def _attention(fast, q, kt, v, side=()):
    hh, t, _ = q.shape
    tq = min(ATT_TQ, t)
    tc = min(ATT_TC, t)
    unroll = min(ATT_UNROLL, t // tc)
    once = pl.Buffered(1)
    side_specs = [pl.BlockSpec(blk, (lambda h, i, f, im=im: im(h, i))) for _, blk, im in side]
    outs = pl.pallas_call(
        functools.partial(_attn_kernel, tc=tc, unroll=unroll, n_side=len(side)),
        grid_spec=pltpu.PrefetchScalarGridSpec(
            num_scalar_prefetch=1,
            grid=(hh, t // tq),
            in_specs=[
                pl.BlockSpec((1, tq, QK_PAD), lambda h, i, f: (h, i, 0)),
                pl.BlockSpec((1, QK_PAD, t), lambda h, i, f: (h, 0, 0), pipeline_mode=once),
                pl.BlockSpec((1, t, V_PAD), lambda h, i, f: (h, 0, 0), pipeline_mode=once),
            ] + side_specs,
            out_specs=[pl.BlockSpec((tq, V_HEAD_DIM), lambda h, i, f: (i, h))] + side_specs,
            scratch_shapes=[pltpu.VMEM((tq, V_PAD), F32), pltpu.VMEM((tq, 1), F32)],
        ),
        out_shape=[jax.ShapeDtypeStruct((t, hh * V_HEAD_DIM), BF16)]
        + [jax.ShapeDtypeStruct(w.shape, BF16) for w, _, _ in side],
        compiler_params=_cparams(("arbitrary", "arbitrary"), vmem=ATT_VMEM_LIMIT_BYTES),
        name="attention",
    )(fast, q, kt, v, *[w for w, _, _ in side])
    return outs


def _fnet1_kernel(u_ref, f1_ref, twc_ref, tws_ref, zr_ref, zi_ref, ubuf_ref, sem):
    k1, n2, width = zr_ref.shape
    i = pl.program_id(0)
    n_steps = pl.num_programs(0)

    def fetch(step, slot, ii):
        return pltpu.make_async_copy(u_ref.at[:, step * k1 + ii, :], ubuf_ref.at[slot, ii], sem.at[slot])

    @pl.when(i == 0)
    def _():
        for ii in range(k1):
            fetch(0, 0, ii).start()

    @pl.when(i + 1 < n_steps)
    def _():
        for ii in range(k1):
            fetch(i + 1, (i + 1) % 2, ii).start()

    slot = i % 2
    for ii in range(k1):
        fetch(i, slot, ii).wait()
    for ii in range(k1):
        y = _dot(f1_ref[...], ubuf_ref[slot, ii].astype(BF16))
        a = y[0:n2]
        b = y[n2:]
        c = twc_ref[ii]
        s = tws_ref[ii]
        zr_ref[ii] = (a * c - b * s).astype(zr_ref.dtype)
        zi_ref[ii] = (-(b * c) - a * s).astype(zi_ref.dtype)


def _fnet2_kernel(zr_ref, zi_ref, f2_ref, cs_ref, wf_ref, o_ref):
    n1, k2, _ = o_ref.shape
    gd = FNET_GROUP_DIM
    zz = jnp.concatenate([zr_ref[...], zi_ref[...]], axis=0)
    x = _dot(f2_ref[...], zz)
    xr = x[0:n1]
    xi = x[n1:]
    for g in range(FNET_GROUPS):
        rows = []
        for j in range(k2):
            c0 = j * FNET_WIDTH + g * gd
            rows.append(jnp.concatenate([xr[:, c0:c0 + gd], xi[:, c0:c0 + gd]], axis=1))
        lhs = jnp.concatenate(rows, axis=0).astype(BF16)
        fg = _dot(lhs, cs_ref[...]).astype(BF16)
        og = _dot(fg, wf_ref[g])
        for j in range(k2):
            o_ref[:, j, g * gd:(g + 1) * gd] = og[j * n1:(j + 1) * n1].astype(o_ref.dtype)


def _dft_consts(n1, n2):
    n = n1 * n2
    a2 = 2.0 * np.pi * np.outer(np.arange(n2), np.arange(n2)) / n2
    f1 = np.concatenate([np.cos(a2), np.sin(a2)], axis=0) / n2
    atw = 2.0 * np.pi * np.outer(np.arange(n1), np.arange(n2)) / n
    a1 = 2.0 * np.pi * np.outer(np.arange(n1), np.arange(n1)) / n1
    c1, s1 = np.cos(a1), np.sin(a1)
    f2 = np.block([[c1, s1], [-s1, c1]])
    ac = 2.0 * np.pi * np.outer(np.arange(FNET_GROUP_DIM), np.arange(FNET_GROUP_DIM)) / FNET_GROUP_DIM
    norm = n2 / math.sqrt(n * FNET_GROUP_DIM)
    cs = np.concatenate([np.cos(ac), np.sin(ac)], axis=0) * norm
    return (jnp.asarray(f1, BF16), jnp.asarray(np.cos(atw)[:, :, None], F32), jnp.asarray(np.sin(atw)[:, :, None], F32),
            jnp.asarray(f2, BF16), jnp.asarray(cs, BF16))


def _fnet(u, wf_b, n1, n2):
    t, width = u.shape
    assert t == n1 * n2
    f1, twc, tws, f2, cs = _dft_consts(n1, n2)
    k1 = min(FNET_K1, n1)
    k2 = min(FNET_K2, n2)
    const2 = lambda i: (0, 0)
    zr, zi = pl.pallas_call(
        _fnet1_kernel,
        grid=(n1 // k1,),
        in_specs=[
            pl.BlockSpec(memory_space=pl.ANY),
            pl.BlockSpec(f1.shape, const2),
            pl.BlockSpec((k1, n2, 1), lambda i: (i, 0, 0)),
            pl.BlockSpec((k1, n2, 1), lambda i: (i, 0, 0)),
        ],
        out_specs=[pl.BlockSpec((k1, n2, width), lambda i: (i, 0, 0))] * 2,
        out_shape=[jax.ShapeDtypeStruct((n1, n2, width), BF16)] * 2,
        scratch_shapes=[pltpu.VMEM((2, k1, n2, width), u.dtype), pltpu.SemaphoreType.DMA((2,))],
        compiler_params=_cparams(("arbitrary",)),
        name="fnet_stage1",
    )(u.reshape(n2, n1, width), f1, twc, tws)
    out = pl.pallas_call(
        _fnet2_kernel,
        grid=(n2 // k2,),
        in_specs=[
            pl.BlockSpec((n1, k2 * width), lambda i: (0, i)),
            pl.BlockSpec((n1, k2 * width), lambda i: (0, i)),
            pl.BlockSpec(f2.shape, const2),
            pl.BlockSpec(cs.shape, const2),
            pl.BlockSpec(wf_b.shape, lambda i: (0, 0, 0)),
        ],
        out_specs=pl.BlockSpec((n1, k2, width), lambda i: (0, i, 0)),
        out_shape=jax.ShapeDtypeStruct((n1, n2, width), F32),
        compiler_params=_cparams(("arbitrary",)),
        name="fnet_stage2",
    )(zr.reshape(n1, n2 * width), zi.reshape(n1, n2 * width), f2, cs, wf_b)
    return out.reshape(t, width)


def _mix_out_kernel(a_ref, f_ref, x_ref, mod_ref, og_ref, w_out_ref, fg_ref, *rest, moe):
    d = x_ref.shape[1]
    half = a_ref.shape[1]
    an = _rms(a_ref[...].astype(F32), og_ref[:, 0:half])
    fn = _rms(f_ref[...].astype(F32), og_ref[:, half:])
    cat = jnp.concatenate([an, fn], axis=1).astype(BF16)
    o = _dot(cat, w_out_ref[...])
    gate1 = mod_ref[:, 2 * d:3 * d]
    shift2 = mod_ref[:, 3 * d:4 * d]
    scale2 = mod_ref[:, 4 * d:5 * d]
    x1 = x_ref[...] + gate1 * o
    h2 = _rms(x1, fg_ref[...]) * (1.0 + scale2) + shift2
    if not moe:
        x1_ref, h2_ref = rest
        x1_ref[...] = x1
        h2_ref[...] = h2.astype(h2_ref.dtype)
        return
    wr_ref, tri_ref, x1_ref, h2_ref, route_ref, cnt_ref, carry_ref = rest
    x1_ref[...] = x1
    h2_ref[...] = h2

    @pl.when(pl.program_id(0) == 0)
    def _():
        carry_ref[...] = jnp.zeros(carry_ref.shape, F32)

    tm = x1.shape[0]
    h_hi = h2.astype(BF16)
    h_lo = (h2 - h_hi.astype(F32)).astype(BF16)
    hw = _dot(h_hi, wr_ref[...])
    logits = (hw[:, 0:LANES] + hw[:, LANES:]) + _dot(h_lo, wr_ref[:, 0:LANES])
    lane = lax.broadcasted_iota(jnp.int32, (tm, LANES), 1)
    neg = jnp.float32(-jnp.inf)
    logits = jnp.where(lane < N_EXPERTS, logits, neg)
    m1 = jnp.max(logits, axis=-1, keepdims=True)
    i1 = jnp.min(jnp.where(logits == m1, lane, LANES), axis=-1, keepdims=True)
    rest_l = jnp.where(lane == i1, neg, logits)
    m2 = jnp.max(rest_l, axis=-1, keepdims=True)
    i2 = jnp.min(jnp.where(rest_l == m2, lane, LANES), axis=-1, keepdims=True)
    e21 = jnp.exp(m2 - m1)
    w1 = 1.0 / (1.0 + e21)
    w2 = e21 / (1.0 + e21)
    oh1 = jnp.where(lane == i1, 1.0, 0.0)
    oh2 = jnp.where(lane == i2, 1.0, 0.0)
    pre = _dot(tri_ref[...], jnp.concatenate([oh1, oh2], axis=1).astype(BF16))
    cnt1 = jnp.sum(oh1, axis=0, keepdims=True)
    cnt2 = jnp.sum(oh2, axis=0, keepdims=True)
    carry = carry_ref[0:1, :]
    rank1 = jnp.sum(oh1 * (carry + pre[:, 0:LANES]), axis=-1, keepdims=True)
    rank2 = jnp.sum(oh2 * (carry + cnt1 + pre[:, LANES:]), axis=-1, keepdims=True)
    new_carry = carry + cnt1 + cnt2
    carry_ref[...] = jnp.broadcast_to(new_carry, carry_ref.shape)
    cnt_ref[...] = jnp.broadcast_to(new_carry, cnt_ref.shape)
    vals = (i1.astype(F32), i2.astype(F32), w1, w2, rank1, rank2)
    route = jnp.zeros((tm, LANES), F32)
    for k, val in enumerate(vals):
        route = jnp.where(lane == k, val, route)
    route_ref[...] = route


def _mix_out(a, f, x2d, mod_l, og, w_out_b, fg, router=None):
    t, d = x2d.shape
    tm = min(MIX_TM, t)
    half = a.shape[1]
    moe = router is not None
    const2 = lambda i: (0, 0)
    row = lambda w: pl.BlockSpec((tm, w), lambda i: (i, 0))
    ins = [a, f, x2d, mod_l, og, w_out_b, fg]
    in_specs = [row(half), row(half), row(d), pl.BlockSpec(mod_l.shape, const2), pl.BlockSpec(og.shape, const2),
                pl.BlockSpec(w_out_b.shape, const2), pl.BlockSpec(fg.shape, const2)]
    out_specs = [row(d), row(d)]
    out_shape = [jax.ShapeDtypeStruct((t, d), F32), jax.ShapeDtypeStruct((t, d), F32 if moe else BF16)]
    scratch = []
    if moe:
        tri = jnp.asarray(np.tril(np.ones((tm, tm), np.float32), -1), BF16)
        ins += [router, tri]
        in_specs += [pl.BlockSpec(router.shape, const2), pl.BlockSpec(tri.shape, const2)]
        out_specs += [row(LANES), pl.BlockSpec((8, LANES), const2)]
        out_shape += [jax.ShapeDtypeStruct((t, LANES), F32), jax.ShapeDtypeStruct((8, LANES), F32)]
        scratch = [pltpu.VMEM((8, LANES), F32)]
    return pl.pallas_call(
        functools.partial(_mix_out_kernel, moe=moe),
        grid=(t // tm,),
        in_specs=in_specs,
        out_specs=out_specs,
        out_shape=out_shape,
        scratch_shapes=scratch,
        compiler_params=_cparams(("arbitrary",)),
        name="mix_out_moe" if moe else "mix_out",
    )(*ins)


def _silu(g):
    return g * (1.0 / (1.0 + jnp.exp(-g)))


def _ffn_kernel(h_ref, x1_ref, mod_ref, wg_ref, wu_ref, wd_ref, o_ref):
    j = pl.program_id(1)
    d = x1_ref.shape[1]

    @pl.when(j == 0)
    def _():
        o_ref[...] = jnp.zeros(o_ref.shape, F32)

    h = h_ref[...]
    act = (_silu(_dot(h, wg_ref[...])) * _dot(h, wu_ref[...])).astype(BF16)
    o_ref[...] += _dot(act, wd_ref[...])

    @pl.when(j == pl.num_programs(1) - 1)
    def _():
        o_ref[...] = x1_ref[...] + mod_ref[:, 5 * d:6 * d] * o_ref[...]


def _ffn_dense(h2, x1, mod_l, wg, wu, wd):
    t, d = x1.shape
    ff = wg.shape[1]
    tm = min(FFN_TM, t)
    tf = min(FFN_TF, ff)
    return pl.pallas_call(
        _ffn_kernel,
        grid=(t // tm, ff // tf),
        in_specs=[
            pl.BlockSpec((tm, d), lambda i, j: (i, 0)),
            pl.BlockSpec((tm, d), lambda i, j: (i, 0)),
            pl.BlockSpec(mod_l.shape, lambda i, j: (0, 0)),
            pl.BlockSpec((d, tf), lambda i, j: (0, j)),
            pl.BlockSpec((d, tf), lambda i, j: (0, j)),
            pl.BlockSpec((tf, d), lambda i, j: (j, 0)),
        ],
        out_specs=pl.BlockSpec((tm, d), lambda i, j: (i, 0)),
        out_shape=jax.ShapeDtypeStruct((t, d), F32),
        compiler_params=_cparams(("arbitrary", "arbitrary")),
        name="ffn_dense",
    )(h2, x1, mod_l, wg, wu, wd)


def _dispatch_kernel(d1_ref, d2_ref, tail_ref, h_ref, xs_ref, zero_ref, sem, zsem):
    tm = h_ref.shape[0]
    base = pl.program_id(0) * tm
    blk = zero_ref.shape[0]

    @pl.when(pl.program_id(0) == 0)
    def _():
        zero_ref[...] = jnp.zeros(zero_ref.shape, zero_ref.dtype)

        def fill(e):
            return pltpu.make_async_copy(zero_ref, xs_ref.at[pl.ds(pl.multiple_of(tail_ref[e], blk), blk), :], zsem)

        for e in range(tail_ref.shape[0]):
            @pl.when(tail_ref[e] >= 0)
            def _():
                fill(e).start()

        for e in range(tail_ref.shape[0]):
            @pl.when(tail_ref[e] >= 0)
            def _():
                fill(e).wait()

    def copy(r, dst):
        return pltpu.make_async_copy(h_ref.at[pl.ds(r, 1), :], xs_ref.at[pl.ds(dst, 1), :], sem)

    def start(r, c):
        copy(r, d1_ref[base + r]).start()
        copy(r, d2_ref[base + r]).start()
        return c

    def wait(r, c):
        copy(r, d1_ref[base + r]).wait()
        copy(r, d2_ref[base + r]).wait()
        return c

    lax.fori_loop(0, tm, start, 0, unroll=ROW_DMA_UNROLL)
    lax.fori_loop(0, tm, wait, 0, unroll=ROW_DMA_UNROLL)


def _dispatch(dest1, dest2, tail, h2, n_rows, blk):
    t, d = h2.shape
    tm = min(ROW_TM, t)
    return pl.pallas_call(
        _dispatch_kernel,
        grid_spec=pltpu.PrefetchScalarGridSpec(
            num_scalar_prefetch=3,
            grid=(t // tm,),
            in_specs=[pl.BlockSpec((tm, d), lambda i, d1, d2, tl: (i, 0))],
            out_specs=pl.BlockSpec(memory_space=pl.ANY),
            scratch_shapes=[pltpu.VMEM((blk, d), h2.dtype), pltpu.SemaphoreType.DMA(()), pltpu.SemaphoreType.DMA(())],
        ),
        out_shape=jax.ShapeDtypeStruct((n_rows, d), h2.dtype),
        compiler_params=_cparams(("arbitrary",)),
        name="moe_dispatch",
    )(dest1, dest2, tail, h2)


def _expert_kernel(be_ref, nu_ref, x_ref, wg_ref, wu_ref, wd_ref, o_ref, xb_ref):
    b = pl.program_id(0)
    j = pl.program_id(1)

    @pl.when(j == 0)
    def _():
        o_ref[...] = jnp.zeros(o_ref.shape, o_ref.dtype)
        xb_ref[...] = x_ref[...].astype(BF16)

    @pl.when(b < nu_ref[0])
    def _():
        xb = xb_ref[...]
        act = (_silu(_dot(xb, wg_ref[0])) * _dot(xb, wu_ref[0])).astype(BF16)
        o_ref[...] += _dot(act, wd_ref[0])


def _experts(block_e, n_used, xs, wg, wu, wd):
    n_rows, d = xs.shape
    ne, _, ff = wg.shape
    blk = min(MOE_BLK, n_rows)
    tf = min(MOE_TF, ff)
    nb = n_rows // blk
    nj = ff // tf

    def last_live(b, nu):
        return jnp.maximum(jnp.minimum(b, nu[0] - 1), 0)

    def row_map(b, j, be, nu):
        return (last_live(b, nu), 0)

    def col_map(b, j, be, nu):
        return (be[last_live(b, nu)], 0, jnp.where(b < nu[0], j, nj - 1))

    def down_map(b, j, be, nu):
        return (be[last_live(b, nu)], jnp.where(b < nu[0], j, nj - 1), 0)

    return pl.pallas_call(
        _expert_kernel,
        grid_spec=pltpu.PrefetchScalarGridSpec(
            num_scalar_prefetch=2,
            grid=(nb, nj),
            in_specs=[
                pl.BlockSpec((blk, d), row_map),
                pl.BlockSpec((1, d, tf), col_map),
                pl.BlockSpec((1, d, tf), col_map),
                pl.BlockSpec((1, tf, d), down_map),
            ],
            out_specs=pl.BlockSpec((blk, d), lambda b, j, be, nu: (b, 0)),
            scratch_shapes=[pltpu.VMEM((blk, d), BF16)],
        ),
        out_shape=jax.ShapeDtypeStruct((n_rows, d), F32),
        compiler_params=_cparams(("arbitrary", "arbitrary")),
        name="moe_experts",
    )(block_e, n_used, xs, wg, wu, wd)


def _combine_kernel(d1_ref, d2_ref, yb_ref, route_ref, x1_ref, mod_ref, o_ref, buf_ref, sem):
    tm, d = x1_ref.shape
    i = pl.program_id(0)

    def copy(step, slot, r):
        par = step % 2
        idx = (d1_ref, d2_ref)[slot][step * tm + r]
        return pltpu.make_async_copy(yb_ref.at[pl.ds(idx, 1), :], buf_ref.at[par, slot, pl.ds(r, 1), :], sem.at[par])

    def issue(step):
        def start(r, c):
            copy(step, 0, r).start()
            copy(step, 1, r).start()
            return c
        lax.fori_loop(0, tm, start, 0, unroll=ROW_DMA_UNROLL)

    @pl.when(i == 0)
    def _():
        issue(0)

    @pl.when(i + 1 < pl.num_programs(0))
    def _():
        issue(i + 1)

    def wait(r, c):
        copy(i, 0, r).wait()
        copy(i, 1, r).wait()
        return c

    lax.fori_loop(0, tm, wait, 0, unroll=ROW_DMA_UNROLL)
    w1 = route_ref[:, 2:3]
    w2 = route_ref[:, 3:4]
    par = i % 2
    y = buf_ref[par, 0] * w1 + buf_ref[par, 1] * w2
    o_ref[...] = x1_ref[...] + mod_ref[:, 5 * d:6 * d] * y


def _combine(dest1, dest2, yb, route, x1, mod_l):
    t, d = x1.shape
    tm = min(ROW_TM, t)
    row = lambda w: pl.BlockSpec((tm, w), lambda i, d1, d2: (i, 0))
    return pl.pallas_call(
        _combine_kernel,
        grid_spec=pltpu.PrefetchScalarGridSpec(
            num_scalar_prefetch=2,
            grid=(t // tm,),
            in_specs=[pl.BlockSpec(memory_space=pl.ANY), row(LANES), row(d),
                      pl.BlockSpec(mod_l.shape, lambda i, d1, d2: (0, 0))],
            out_specs=row(d),
            scratch_shapes=[pltpu.VMEM((2, 2, tm, d), F32), pltpu.SemaphoreType.DMA((2,))],
        ),
        out_shape=jax.ShapeDtypeStruct((t, d), F32),
        compiler_params=_cparams(("arbitrary",)),
        name="moe_combine",
    )(dest1, dest2, yb, route, x1, mod_l)


def _moe_plan(route, cnt, t, blk):
    e1 = route[:, 0].astype(jnp.int32)
    e2 = route[:, 1].astype(jnp.int32)
    r1 = route[:, 4].astype(jnp.int32)
    r2 = route[:, 5].astype(jnp.int32)
    counts = cnt[0, :N_EXPERTS].astype(jnp.int32)
    padded = (counts + blk - 1) // blk * blk
    pend = jnp.cumsum(padded)
    pstart = pend - padded
    eids = jnp.arange(N_EXPERTS, dtype=jnp.int32)
    dest1 = jnp.sum(jnp.where(e1[:, None] == eids[None, :], pstart[None, :], 0), axis=1) + r1
    dest2 = jnp.sum(jnp.where(e2[:, None] == eids[None, :], pstart[None, :], 0), axis=1) + r2
    n_blocks = -(-(2 * t) // blk) + N_EXPERTS
    bstart = jnp.arange(n_blocks, dtype=jnp.int32) * blk
    block_e = jnp.minimum(jnp.sum((pend[None, :] <= bstart[:, None]).astype(jnp.int32), axis=1), N_EXPERTS - 1)
    n_used = (pend[-1] // blk).astype(jnp.int32).reshape(1)
    tail = jnp.where(padded > 0, pend - blk, -1)
    spare = bstart[(2 * t) // blk:]
    tail = jnp.concatenate([tail, jnp.where(spare >= pend[-1], spare, -1)]).astype(jnp.int32)
    return (dest1.astype(jnp.int32), dest2.astype(jnp.int32), block_e.astype(jnp.int32), n_used, tail,
            n_blocks * blk)


def _prep_mixer_weights(w_in, cq_norm, ckv_norm, w_uq, w_ukv, q_norm, k_norm):
    s1 = Q_LORA_RANK
    s2 = s1 + KV_LORA_RANK
    s3 = s2 + QK_ROPE_DIM
    half = QK_ROPE_DIM // 2
    swap = np.concatenate([np.arange(half, QK_ROPE_DIM), np.arange(half)])
    w_kr = w_in[:, s2:s3]
    uq = w_uq.reshape(Q_LORA_RANK, MLA_HEADS, QK_HEAD_DIM)
    uq_rope = uq[:, :, QK_NOPE_DIM:]
    ukv = w_ukv.reshape(KV_LORA_RANK, MLA_HEADS, QK_NOPE_DIM + V_HEAD_DIM)
    qg_r = q_norm[QK_NOPE_DIM:]
    kg_r = k_norm[QK_NOPE_DIM:]
    reps = LANES // QK_ROPE_DIM
    return {
        "w_in": jnp.concatenate([w_in[:, :s2], w_in[:, s3:]], axis=1).astype(BF16),
        "w_kr": jnp.concatenate([w_kr, w_kr[:, swap]], axis=1).T.astype(BF16),
        "cq_g": cq_norm.reshape(1, -1),
        "ckv_g": ckv_norm.reshape(1, -1),
        "w_uq": jnp.concatenate([uq[:, :, :QK_NOPE_DIM].reshape(Q_LORA_RANK, -1),
                                 uq_rope.reshape(Q_LORA_RANK, -1),
                                 uq_rope[:, :, swap].reshape(Q_LORA_RANK, -1)], axis=1).astype(BF16),
        "w_uv": ukv[:, :, QK_NOPE_DIM:].reshape(KV_LORA_RANK, -1).astype(BF16),
        "w_uk": ukv[:, :, :QK_NOPE_DIM].reshape(KV_LORA_RANK, -1).T.astype(BF16),
        "qg_n": q_norm[:QK_NOPE_DIM].reshape(1, -1),
        "qg_r": jnp.tile(qg_r, reps).reshape(1, -1),
        "qg_s": jnp.tile(qg_r[swap], reps).reshape(1, -1),
        "kg_n": k_norm[:QK_NOPE_DIM].reshape(-1, 1),
        "kg_r": kg_r.reshape(-1, 1),
        "kg_s": kg_r[swap].reshape(-1, 1),
    }


BF16_ROWS = 16


def _row_blocked(w, n_qt):
    steps = MLA_HEADS * n_qt
    tiles = w.shape[0] // BF16_ROWS
    n_blk = max(k for k in range(1, steps + 1) if tiles % k == 0)
    return w, (w.shape[0] // n_blk, w.shape[1]), lambda h, i: (jnp.minimum(h * n_qt + i, n_blk - 1), 0)


def _expert_blocked(w, n_qt):
    assert w.shape[0] == MLA_HEADS and w.shape[1] % (n_qt * BF16_ROWS) == 0
    return w, (1, w.shape[1] // n_qt, w.shape[2]), lambda h, i: (h, i, 0)


def _fnet_factors(t):
    n2 = 1 << (int(math.log2(t)) // 2)
    return t // n2, n2


def kernel(x, c, positions, ada_w, ada_b, mix_norm, w_in, cq_norm, ckv_norm, w_uq, w_ukv, q_norm, k_norm, w_fnet,
           out_norm, w_out, ffn_norm, dense_w_gate, dense_w_up, dense_w_down, router_w, moe_w_gate, moe_w_up,
           moe_w_down):
    b, s, d = x.shape
    assert b == 1
    t = b * s
    depth = ada_w.shape[0]
    assert depth == 2 and dense_w_gate.shape[0] == 1 and moe_w_gate.shape[0] == 1
    x2d = x.reshape(t, d)
    mod = _ada_mod(c.reshape(d, 1), ada_w, ada_b.reshape(depth, 1, -1))
    cos, sin = _rope_tables(positions.reshape(t).astype(F32))
    n1, n2 = _fnet_factors(t)
    n_qt = t // min(ATT_TQ, t)
    side_jobs = [
        [_row_blocked(dense_w_gate[0], n_qt), _row_blocked(dense_w_up[0], n_qt), _row_blocked(dense_w_down[0], n_qt),
         _expert_blocked(moe_w_gate[0], n_qt)],
        [_expert_blocked(moe_w_up[0], n_qt), _expert_blocked(moe_w_down[0], n_qt)],
    ]
    cast = []
    for l in range(depth):
        mod_l = mod[l]
        wp = _prep_mixer_weights(w_in[l], cq_norm[l], ckv_norm[l], w_uq[l], w_ukv[l], q_norm[l], k_norm[l])
        fast = (_score_bound(q_norm[l], k_norm[l]) <= ATT_FAST_BOUND).astype(jnp.int32).reshape(1)
        q, kt, v, u = _mix_in(x2d, mod_l, mix_norm[l].reshape(1, d), wp, cos, sin)
        a, *new_cast = _attention(fast, q, kt, v, side_jobs[l])
        cast += new_cast
        f = _fnet(u, w_fnet[l].astype(BF16), n1, n2)
        og = out_norm[l].reshape(1, d)
        fg = ffn_norm[l].reshape(1, d)
        w_out_b = w_out[l].astype(BF16)
        j = l // 2
        if l % 2 == 0:
            x1, h2 = _mix_out(a, f, x2d, mod_l, og, w_out_b, fg)
            x2d = _ffn_dense(h2, x1, mod_l, cast[0], cast[1], cast[2])
        else:
            wr = jnp.pad(router_w[j], ((0, 0), (0, LANES - N_EXPERTS)))
            wr_hi = wr.astype(BF16)
            router = jnp.concatenate([wr_hi, (wr - wr_hi.astype(F32)).astype(BF16)], axis=1)
            x1, h2, route, cnt = _mix_out(a, f, x2d, mod_l, og, w_out_b, fg, router=router)
            dest1, dest2, block_e, n_used, tail, n_rows = _moe_plan(route, cnt, t, MOE_BLK)
            xs = _dispatch(dest1, dest2, tail, h2, n_rows, MOE_BLK)
            yb = _experts(block_e, n_used, xs, cast[3], cast[4], cast[5])
            x2d = _combine(dest1, dest2, yb, route, x1, mod_l)
    return x2d.reshape(b, s, d)
```

```python
import functools
import math

import numpy as np
import jax
import jax.numpy as jnp
from jax import lax
from jax.experimental import pallas as pl
from jax.experimental.pallas import tpu as pltpu

D_MODEL = 2048
DEPTH = 2
MLA_HEADS = 8
QK_NOPE_DIM = 128
QK_ROPE_DIM = 64
V_HEAD_DIM = 128
QK_HEAD_DIM = QK_NOPE_DIM + QK_ROPE_DIM
Q_LORA_RANK = 512
KV_LORA_RANK = 256
MLA_WIDTH = MLA_HEADS * V_HEAD_DIM
FNET_GROUPS = 8
FNET_GROUP_DIM = 128
FNET_WIDTH = FNET_GROUPS * FNET_GROUP_DIM
ROPE_THETA = 10000.0
N_EXPERTS = 8
N_MOD = 6
RMS_EPS = 1e-6

LANES = 128
QK_PAD = 256
V_PAD = 256
VMEM_LIMIT_BYTES = 56 * 1024 * 1024
ATT_VMEM_LIMIT_BYTES = 60 * 1024 * 1024

ADA_TN = 1024
ROPE_TN = 2048
MIX_TM = 512
ATT_TQ = 1024
ATT_TC = 1024
ATT_UNROLL = 8
FNET_K1 = 4
FNET_K2 = 8
FFN_TM = 512
FFN_TF = 512
MOE_BLK = 512
MOE_TF = 1024
ROW_TM = 512
ROW_DMA_UNROLL = 8

BF16 = jnp.bfloat16
F32 = jnp.float32
NT_DIMS = (((1,), (1,)), ((), ()))


def _cparams(sem, vmem=VMEM_LIMIT_BYTES):
    return pltpu.CompilerParams(dimension_semantics=sem, vmem_limit_bytes=vmem)


def _dot(a, b):
    return jnp.dot(a, b, preferred_element_type=F32)


def _rms(x, g):
    ms = jnp.mean(x * x, axis=-1, keepdims=True)
    return (x * lax.rsqrt(ms + RMS_EPS)) * g


def _ada_kernel(c_ref, w_ref, b_ref, o_ref, cond_ref):
    d = w_ref.shape[1]
    tn = w_ref.shape[2]
    rows = 64

    @pl.when(jnp.logical_and(pl.program_id(0) == 0, pl.program_id(1) == 0))
    def _():
        c = c_ref[...]
        cond_ref[...] = jnp.broadcast_to(c * (1.0 / (1.0 + jnp.exp(-c))), cond_ref.shape)

    def body(i, acc):
        r = pl.multiple_of(i * rows, rows)
        cond = cond_ref[pl.ds(r, rows), :]
        for s in range(rows // 8):
            cs = cond[8 * s:8 * s + 8]
            w8 = w_ref[0, pl.ds(r + 8 * s, 8), :]
            acc = acc + w8 * jnp.concatenate([cs] * (tn // LANES), axis=1)
        return acc

    acc = lax.fori_loop(0, d // rows, body, jnp.zeros((8, tn), F32))
    o_ref[0] = jnp.sum(acc, axis=0, keepdims=True) + b_ref[0]


def _ada_mod(c_col, ada_w, ada_b3):
    depth, d, n = ada_w.shape
    tn = min(ADA_TN, n)
    return pl.pallas_call(
        _ada_kernel,
        grid=(depth, n // tn),
        in_specs=[
            pl.BlockSpec((d, 1), lambda l, j: (0, 0)),
            pl.BlockSpec((1, d, tn), lambda l, j: (l, 0, j)),
            pl.BlockSpec((1, 1, tn), lambda l, j: (l, 0, j)),
        ],
        out_specs=pl.BlockSpec((1, 1, tn), lambda l, j: (l, 0, j)),
        out_shape=jax.ShapeDtypeStruct((depth, 1, n), F32),
        scratch_shapes=[pltpu.VMEM((d, LANES), F32)],
        compiler_params=_cparams(("arbitrary", "arbitrary")),
        name="ada_mod",
    )(c_col, ada_w, ada_b3)


def _rope_kernel(pos_ref, invf_ref, cos_ref, sin_ref):
    ang = invf_ref[...] * pos_ref[...]
    cos_ref[...] = jnp.cos(ang)
    sin_ref[...] = jnp.sin(ang)


def _rope_tables(pos_f32):
    t = pos_f32.shape[0]
    tn = min(ROPE_TN, t)
    half = QK_ROPE_DIM // 2
    inv_freq = ROPE_THETA ** (-jnp.arange(half, dtype=F32) / half)
    return pl.pallas_call(
        _rope_kernel,
        grid=(t // tn,),
        in_specs=[pl.BlockSpec((1, tn), lambda i: (0, i)), pl.BlockSpec((half, 1), lambda i: (0, 0))],
        out_specs=[pl.BlockSpec((half, tn), lambda i: (0, i))] * 2,
        out_shape=[jax.ShapeDtypeStruct((half, t), F32)] * 2,
        compiler_params=_cparams(("arbitrary",)),
        name="rope_tables",
    )(pos_f32.reshape(1, t), inv_freq.reshape(half, 1))


Q_SCALE = (QK_HEAD_DIM ** -0.5) * math.log2(math.e)


def _mix_in_kernel(x_ref, mod_ref, ng_ref, w_in_ref, w_kr_ref, cqg_ref, ckvg_ref, w_uq_ref, w_uv_ref, w_uk_ref,
                   qgn_ref, qgr_ref, qgs_ref, kgn_ref, kgr_ref, kgs_ref, cos_ref, sin_ref,
                   q_ref, kt_ref, v_ref, u_ref):
    d = x_ref.shape[1]
    cos = cos_ref[...]
    sin = sin_ref[...]
    cos_t = jnp.concatenate([cos, cos], axis=0)
    sin_t = jnp.concatenate([-sin, sin], axis=0)
    cos4 = jnp.concatenate([cos_t, cos_t], axis=0).T
    sin4 = jnp.concatenate([sin_t, sin_t], axis=0).T
    x = x_ref[...]
    shift = mod_ref[:, 0:d]
    scale = mod_ref[:, d:2 * d]
    h = _rms(x, ng_ref[...]) * (1.0 + scale) + shift
    hb = h.astype(BF16)
    z = _dot(hb, w_in_ref[...])
    c_q = z[:, 0:Q_LORA_RANK]
    c_kv = z[:, Q_LORA_RANK:Q_LORA_RANK + KV_LORA_RANK]
    u_ref[...] = z[:, Q_LORA_RANK + KV_LORA_RANK:]
    cqn = _rms(c_q, cqg_ref[...]).astype(BF16)
    ckvn = _rms(c_kv, ckvg_ref[...]).astype(BF16)

    kn_t = lax.dot_general(w_uk_ref[...], ckvn, NT_DIMS, preferred_element_type=F32)
    kr_t = lax.dot_general(w_kr_ref[...], hb, NT_DIMS, preferred_element_type=F32)
    kr = kr_t[0:QK_ROPE_DIM]
    kr_rot = (kr * kgr_ref[...]) * cos_t + (kr_t[QK_ROPE_DIM:] * kgs_ref[...]) * sin_t
    ss_r = jnp.sum(kr * kr, axis=0, keepdims=True)
    tm = x.shape[0]
    pad_rows = jnp.zeros((QK_PAD - QK_HEAD_DIM, tm), BF16)
    for hd in range(MLA_HEADS):
        kn = kn_t[hd * QK_NOPE_DIM:(hd + 1) * QK_NOPE_DIM]
        ss = jnp.sum(kn * kn, axis=0, keepdims=True) + ss_r
        r = lax.rsqrt(ss * (1.0 / QK_HEAD_DIM) + RMS_EPS)
        kt_ref[hd, 0:QK_NOPE_DIM, :] = ((kn * r) * kgn_ref[...]).astype(BF16)
        kt_ref[hd, QK_NOPE_DIM:QK_HEAD_DIM, :] = (kr_rot * r).astype(BF16)
        kt_ref[hd, QK_HEAD_DIM:, :] = pad_rows

    vz = _dot(ckvn, w_uv_ref[...])
    ones = jnp.ones((tm, V_PAD - V_HEAD_DIM), BF16)
    for hd in range(MLA_HEADS):
        v_ref[hd, :, 0:V_HEAD_DIM] = vz[:, hd * V_HEAD_DIM:(hd + 1) * V_HEAD_DIM].astype(BF16)
        v_ref[hd, :, V_HEAD_DIM:] = ones

    qz = _dot(cqn, w_uq_ref[...])
    n_nope = MLA_HEADS * QK_NOPE_DIM
    n_rope = MLA_HEADS * QK_ROPE_DIM
    lane = lax.broadcasted_iota(jnp.int32, (x.shape[0], LANES), 1)
    for pair in range(MLA_HEADS // 2):
        pr = qz[:, n_nope + pair * LANES:n_nope + (pair + 1) * LANES]
        ps = qz[:, n_nope + n_rope + pair * LANES:n_nope + n_rope + (pair + 1) * LANES]
        rot = (pr * qgr_ref[...]) * cos4 + (ps * qgs_ref[...]) * sin4
        pr2 = pr * pr
        for hd in (2 * pair, 2 * pair + 1):
            nope = qz[:, hd * QK_NOPE_DIM:(hd + 1) * QK_NOPE_DIM]
            own = (lane < QK_ROPE_DIM) if hd % 2 == 0 else (lane >= QK_ROPE_DIM)
            ss = jnp.sum(nope * nope + jnp.where(own, pr2, 0.0), axis=-1, keepdims=True)
            r = lax.rsqrt(ss * (1.0 / QK_HEAD_DIM) + RMS_EPS) * Q_SCALE
            q_ref[hd, :, 0:QK_NOPE_DIM] = ((nope * r) * qgn_ref[...]).astype(BF16)
            rope = rot * r
            if hd % 2 == 1:
                rope = pltpu.roll(rope, QK_ROPE_DIM, 1)
            q_ref[hd, :, QK_NOPE_DIM:] = jnp.where(lane < QK_ROPE_DIM, rope, 0.0).astype(BF16)


def _mix_in(x2d, mod_l, ng, wp, cos, sin):
    t, d = x2d.shape
    tm = min(MIX_TM, t)
    hh = MLA_HEADS
    const2 = lambda i: (0, 0)
    full = lambda a: pl.BlockSpec(a.shape, const2)
    ins = [x2d, mod_l, ng, wp["w_in"], wp["w_kr"], wp["cq_g"], wp["ckv_g"], wp["w_uq"], wp["w_uv"], wp["w_uk"],
           wp["qg_n"], wp["qg_r"], wp["qg_s"], wp["kg_n"], wp["kg_r"], wp["kg_s"]]
    in_specs = [pl.BlockSpec((tm, d), lambda i: (i, 0))] + [full(a) for a in ins[1:]]
    ins += [cos, sin]
    in_specs += [pl.BlockSpec((QK_ROPE_DIM // 2, tm), lambda i: (0, i))] * 2
    return pl.pallas_call(
        _mix_in_kernel,
        grid=(t // tm,),
        in_specs=in_specs,
        out_specs=[
            pl.BlockSpec((hh, tm, QK_PAD), lambda i: (0, i, 0)),
            pl.BlockSpec((hh, QK_PAD, tm), lambda i: (0, 0, i)),
            pl.BlockSpec((hh, tm, V_PAD), lambda i: (0, i, 0)),
            pl.BlockSpec((tm, FNET_WIDTH), lambda i: (i, 0)),
        ],
        out_shape=[
            jax.ShapeDtypeStruct((hh, t, QK_PAD), BF16),
            jax.ShapeDtypeStruct((hh, QK_PAD, t), BF16),
            jax.ShapeDtypeStruct((hh, t, V_PAD), BF16),
            jax.ShapeDtypeStruct((t, FNET_WIDTH), F32),
        ],
        compiler_params=_cparams(("arbitrary",)),
        name="mix_in",
    )(*ins)


ATT_FAST_BOUND = 48.0


def _score_bound(q_gain, k_gain):
    return (Q_SCALE * QK_HEAD_DIM) * jnp.max(jnp.abs(q_gain)) * jnp.max(jnp.abs(k_gain))


def _attn_kernel(fast_ref, q_ref, kt_ref, v_ref, *rest, tc, unroll, n_side):
    side_in = rest[:n_side]
    o_ref = rest[n_side]
    side_out = rest[n_side + 1:2 * n_side + 1]
    acc_ref, m_ref = rest[2 * n_side + 1:]
    n_chunks = kt_ref.shape[2] // tc
    acc_ref[...] = jnp.zeros(acc_ref.shape, F32)

    for wi_ref, wo_ref in zip(side_in, side_out):
        wo_ref[...] = wi_ref[...].astype(BF16)

    def chunk(c):
        off = pl.multiple_of(c * tc, tc)
        return kt_ref[0, :, pl.ds(off, tc)], v_ref[0, pl.ds(off, tc), :]

    @pl.when(fast_ref[0] == 1)
    def _():
        def body(c2, carry):
            for u in range(unroll):
                k_c, v_c = chunk(unroll * c2 + u)
                p = jnp.exp2(_dot(q_ref[0], k_c)).astype(BF16)
                acc_ref[...] += _dot(p, v_c)
            return carry

        lax.fori_loop(0, n_chunks // unroll, body, 0)

    @pl.when(fast_ref[0] != 1)
    def _():
        m_ref[...] = jnp.full(m_ref.shape, -jnp.inf, F32)

        def body(c, carry):
            k_c, v_c = chunk(c)
            s = _dot(q_ref[0], k_c)
            m_prev = m_ref[...]
            m_new = jnp.maximum(m_prev, jnp.max(s, axis=-1, keepdims=True))
            p = jnp.exp2(s - m_new).astype(BF16)
            acc_ref[...] = jnp.exp2(m_prev - m_new) * acc_ref[...] + _dot(p, v_c)
            m_ref[...] = m_new
            return carry

        lax.fori_loop(0, n_chunks, body, 0)

    o_ref[...] = (acc_ref[:, 0:V_HEAD_DIM] * (1.0 / acc_ref[:, V_HEAD_DIM:])).astype(o_ref.dtype)


def _attention(fast, q, kt, v, side=()):
    hh, t, _ = q.shape
    tq = min(ATT_TQ, t)
    tc = min(ATT_TC, t)
    unroll = min(ATT_UNROLL, t // tc)
    once = pl.Buffered(1)
    side_specs = [pl.BlockSpec(blk, (lambda h, i, f, im=im: im(h, i))) for _, blk, im in side]
    outs = pl.pallas_call(
        functools.partial(_attn_kernel, tc=tc, unroll=unroll, n_side=len(side)),
        grid_spec=pltpu.PrefetchScalarGridSpec(
            num_scalar_prefetch=1,
            grid=(hh, t // tq),
            in_specs=[
                pl.BlockSpec((1, tq, QK_PAD), lambda h, i, f: (h, i, 0)),
                pl.BlockSpec((1, QK_PAD, t), lambda h, i, f: (h, 0, 0), pipeline_mode=once),
                pl.BlockSpec((1, t, V_PAD), lambda h, i, f: (h, 0, 0), pipeline_mode=once),
            ] + side_specs,
            out_specs=[pl.BlockSpec((tq, V_HEAD_DIM), lambda h, i, f: (i, h))] + side_specs,
            scratch_shapes=[pltpu.VMEM((tq, V_PAD), F32), pltpu.VMEM((tq, 1), F32)],
        ),
        out_shape=[jax.ShapeDtypeStruct((t, hh * V_HEAD_DIM), BF16)]
        + [jax.ShapeDtypeStruct(w.shape, BF16) for w, _, _ in side],
        compiler_params=_cparams(("arbitrary", "arbitrary"), vmem=ATT_VMEM_LIMIT_BYTES),
        name="attention",
    )(fast, q, kt, v, *[w for w, _, _ in side])
    return outs


def _fnet1_kernel(u_ref, f1_ref, twc_ref, tws_ref, zr_ref, zi_ref, ubuf_ref, sem):
    k1, n2, width = zr_ref.shape
    i = pl.program_id(0)
    n_steps = pl.num_programs(0)

    def fetch(step, slot, ii):
        return pltpu.make_async_copy(u_ref.at[:, step * k1 + ii, :], ubuf_ref.at[slot, ii], sem.at[slot])

    @pl.when(i == 0)
    def _():
        for ii in range(k1):
            fetch(0, 0, ii).start()

    @pl.when(i + 1 < n_steps)
    def _():
        for ii in range(k1):
            fetch(i + 1, (i + 1) % 2, ii).start()

    slot = i % 2
    for ii in range(k1):
        fetch(i, slot, ii).wait()
    for ii in range(k1):
        y = _dot(f1_ref[...], ubuf_ref[slot, ii].astype(BF16))
        a = y[0:n2]
        b = y[n2:]
        c = twc_ref[ii]
        s = tws_ref[ii]
        zr_ref[ii] = (a * c - b * s).astype(zr_ref.dtype)
        zi_ref[ii] = (-(b * c) - a * s).astype(zi_ref.dtype)


def _fnet2_kernel(zr_ref, zi_ref, f2_ref, cs_ref, wf_ref, o_ref):
    n1, k2, _ = o_ref.shape
    gd = FNET_GROUP_DIM
    zz = jnp.concatenate([zr_ref[...], zi_ref[...]], axis=0)
    x = _dot(f2_ref[...], zz)
    xr = x[0:n1]
    xi = x[n1:]
    for g in range(FNET_GROUPS):
        rows = []
        for j in range(k2):
            c0 = j * FNET_WIDTH + g * gd
            rows.append(jnp.concatenate([xr[:, c0:c0 + gd], xi[:, c0:c0 + gd]], axis=1))
        lhs = jnp.concatenate(rows, axis=0).astype(BF16)
        fg = _dot(lhs, cs_ref[...]).astype(BF16)
        og = _dot(fg, wf_ref[g])
        for j in range(k2):
            o_ref[:, j, g * gd:(g + 1) * gd] = og[j * n1:(j + 1) * n1].astype(o_ref.dtype)


def _dft_consts(n1, n2):
    n = n1 * n2
    a2 = 2.0 * np.pi * np.outer(np.arange(n2), np.arange(n2)) / n2
    f1 = np.concatenate([np.cos(a2), np.sin(a2)], axis=0) / n2
    atw = 2.0 * np.pi * np.outer(np.arange(n1), np.arange(n2)) / n
    a1 = 2.0 * np.pi * np.outer(np.arange(n1), np.arange(n1)) / n1
    c1, s1 = np.cos(a1), np.sin(a1)
    f2 = np.block([[c1, s1], [-s1, c1]])
    ac = 2.0 * np.pi * np.outer(np.arange(FNET_GROUP_DIM), np.arange(FNET_GROUP_DIM)) / FNET_GROUP_DIM
    norm = n2 / math.sqrt(n * FNET_GROUP_DIM)
    cs = np.concatenate([np.cos(ac), np.sin(ac)], axis=0) * norm
    return (jnp.asarray(f1, BF16), jnp.asarray(np.cos(atw)[:, :, None], F32), jnp.asarray(np.sin(atw)[:, :, None], F32),
            jnp.asarray(f2, BF16), jnp.asarray(cs, BF16))


def _fnet(u, wf_b, n1, n2):
    t, width = u.shape
    assert t == n1 * n2
    f1, twc, tws, f2, cs = _dft_consts(n1, n2)
    k1 = min(FNET_K1, n1)
    k2 = min(FNET_K2, n2)
    const2 = lambda i: (0, 0)
    zr, zi = pl.pallas_call(
        _fnet1_kernel,
        grid=(n1 // k1,),
        in_specs=[
            pl.BlockSpec(memory_space=pl.ANY),
            pl.BlockSpec(f1.shape, const2),
            pl.BlockSpec((k1, n2, 1), lambda i: (i, 0, 0)),
            pl.BlockSpec((k1, n2, 1), lambda i: (i, 0, 0)),
        ],
        out_specs=[pl.BlockSpec((k1, n2, width), lambda i: (i, 0, 0))] * 2,
        out_shape=[jax.ShapeDtypeStruct((n1, n2, width), BF16)] * 2,
        scratch_shapes=[pltpu.VMEM((2, k1, n2, width), u.dtype), pltpu.SemaphoreType.DMA((2,))],
        compiler_params=_cparams(("arbitrary",)),
        name="fnet_stage1",
    )(u.reshape(n2, n1, width), f1, twc, tws)
    out = pl.pallas_call(
        _fnet2_kernel,
        grid=(n2 // k2,),
        in_specs=[
            pl.BlockSpec((n1, k2 * width), lambda i: (0, i)),
            pl.BlockSpec((n1, k2 * width), lambda i: (0, i)),
            pl.BlockSpec(f2.shape, const2),
            pl.BlockSpec(cs.shape, const2),
            pl.BlockSpec(wf_b.shape, lambda i: (0, 0, 0)),
        ],
        out_specs=pl.BlockSpec((n1, k2, width), lambda i: (0, i, 0)),
        out_shape=jax.ShapeDtypeStruct((n1, n2, width), F32),
        compiler_params=_cparams(("arbitrary",)),
        name="fnet_stage2",
    )(zr.reshape(n1, n2 * width), zi.reshape(n1, n2 * width), f2, cs, wf_b)
    return out.reshape(t, width)


def _mix_out_kernel(a_ref, f_ref, x_ref, mod_ref, og_ref, w_out_ref, fg_ref, *rest, moe):
    d = x_ref.shape[1]
    half = a_ref.shape[1]
    an = _rms(a_ref[...].astype(F32), og_ref[:, 0:half])
    fn = _rms(f_ref[...].astype(F32), og_ref[:, half:])
    cat = jnp.concatenate([an, fn], axis=1).astype(BF16)
    o = _dot(cat, w_out_ref[...])
    gate1 = mod_ref[:, 2 * d:3 * d]
    shift2 = mod_ref[:, 3 * d:4 * d]
    scale2 = mod_ref[:, 4 * d:5 * d]
    x1 = x_ref[...] + gate1 * o
    h2 = _rms(x1, fg_ref[...]) * (1.0 + scale2) + shift2
    if not moe:
        x1_ref, h2_ref = rest
        x1_ref[...] = x1
        h2_ref[...] = h2.astype(h2_ref.dtype)
        return
    wr_ref, tri_ref, x1_ref, h2_ref, route_ref, cnt_ref, carry_ref = rest
    x1_ref[...] = x1
    h2_ref[...] = h2

    @pl.when(pl.program_id(0) == 0)
    def _():
        carry_ref[...] = jnp.zeros(carry_ref.shape, F32)

    tm = x1.shape[0]
    h_hi = h2.astype(BF16)
    h_lo = (h2 - h_hi.astype(F32)).astype(BF16)
    hw = _dot(h_hi, wr_ref[...])
    logits = (hw[:, 0:LANES] + hw[:, LANES:]) + _dot(h_lo, wr_ref[:, 0:LANES])
    lane = lax.broadcasted_iota(jnp.int32, (tm, LANES), 1)
    neg = jnp.float32(-jnp.inf)
    logits = jnp.where(lane < N_EXPERTS, logits, neg)
    m1 = jnp.max(logits, axis=-1, keepdims=True)
    i1 = jnp.min(jnp.where(logits == m1, lane, LANES), axis=-1, keepdims=True)
    rest_l = jnp.where(lane == i1, neg, logits)
    m2 = jnp.max(rest_l, axis=-1, keepdims=True)
    i2 = jnp.min(jnp.where(rest_l == m2, lane, LANES), axis=-1, keepdims=True)
    e21 = jnp.exp(m2 - m1)
    w1 = 1.0 / (1.0 + e21)
    w2 = e21 / (1.0 + e21)
    oh1 = jnp.where(lane == i1, 1.0, 0.0)
    oh2 = jnp.where(lane == i2, 1.0, 0.0)
    pre = _dot(tri_ref[...], jnp.concatenate([oh1, oh2], axis=1).astype(BF16))
    cnt1 = jnp.sum(oh1, axis=0, keepdims=True)
    cnt2 = jnp.sum(oh2, axis=0, keepdims=True)
    carry = carry_ref[0:1, :]
    rank1 = jnp.sum(oh1 * (carry + pre[:, 0:LANES]), axis=-1, keepdims=True)
    rank2 = jnp.sum(oh2 * (carry + cnt1 + pre[:, LANES:]), axis=-1, keepdims=True)
    new_carry = carry + cnt1 + cnt2
    carry_ref[...] = jnp.broadcast_to(new_carry, carry_ref.shape)
    cnt_ref[...] = jnp.broadcast_to(new_carry, cnt_ref.shape)
    vals = (i1.astype(F32), i2.astype(F32), w1, w2, rank1, rank2)
    route = jnp.zeros((tm, LANES), F32)
    for k, val in enumerate(vals):
        route = jnp.where(lane == k, val, route)
    route_ref[...] = route


def _mix_out(a, f, x2d, mod_l, og, w_out_b, fg, router=None):
    t, d = x2d.shape
    tm = min(MIX_TM, t)
    half = a.shape[1]
    moe = router is not None
    const2 = lambda i: (0, 0)
    row = lambda w: pl.BlockSpec((tm, w), lambda i: (i, 0))
    ins = [a, f, x2d, mod_l, og, w_out_b, fg]
    in_specs = [row(half), row(half), row(d), pl.BlockSpec(mod_l.shape, const2), pl.BlockSpec(og.shape, const2),
                pl.BlockSpec(w_out_b.shape, const2), pl.BlockSpec(fg.shape, const2)]
    out_specs = [row(d), row(d)]
    out_shape = [jax.ShapeDtypeStruct((t, d), F32), jax.ShapeDtypeStruct((t, d), F32 if moe else BF16)]
    scratch = []
    if moe:
        tri = jnp.asarray(np.tril(np.ones((tm, tm), np.float32), -1), BF16)
        ins += [router, tri]
        in_specs += [pl.BlockSpec(router.shape, const2), pl.BlockSpec(tri.shape, const2)]
        out_specs += [row(LANES), pl.BlockSpec((8, LANES), const2)]
        out_shape += [jax.ShapeDtypeStruct((t, LANES), F32), jax.ShapeDtypeStruct((8, LANES), F32)]
        scratch = [pltpu.VMEM((8, LANES), F32)]
    return pl.pallas_call(
        functools.partial(_mix_out_kernel, moe=moe),
        grid=(t // tm,),
        in_specs=in_specs,
        out_specs=out_specs,
        out_shape=out_shape,
        scratch_shapes=scratch,
        compiler_params=_cparams(("arbitrary",)),
        name="mix_out_moe" if moe else "mix_out",
    )(*ins)


def _silu(g):
    return g * (1.0 / (1.0 + jnp.exp(-g)))


def _ffn_kernel(h_ref, x1_ref, mod_ref, wg_ref, wu_ref, wd_ref, o_ref):
    j = pl.program_id(1)
    d = x1_ref.shape[1]

    @pl.when(j == 0)
    def _():
        o_ref[...] = jnp.zeros(o_ref.shape, F32)

    h = h_ref[...]
    act = (_silu(_dot(h, wg_ref[...])) * _dot(h, wu_ref[...])).astype(BF16)
    o_ref[...] += _dot(act, wd_ref[...])

    @pl.when(j == pl.num_programs(1) - 1)
    def _():
        o_ref[...] = x1_ref[...] + mod_ref[:, 5 * d:6 * d] * o_ref[...]


def _ffn_dense(h2, x1, mod_l, wg, wu, wd):
    t, d = x1.shape
    ff = wg.shape[1]
    tm = min(FFN_TM, t)
    tf = min(FFN_TF, ff)
    return pl.pallas_call(
        _ffn_kernel,
        grid=(t // tm, ff // tf),
        in_specs=[
            pl.BlockSpec((tm, d), lambda i, j: (i, 0)),
            pl.BlockSpec((tm, d), lambda i, j: (i, 0)),
            pl.BlockSpec(mod_l.shape, lambda i, j: (0, 0)),
            pl.BlockSpec((d, tf), lambda i, j: (0, j)),
            pl.BlockSpec((d, tf), lambda i, j: (0, j)),
            pl.BlockSpec((tf, d), lambda i, j: (j, 0)),
        ],
        out_specs=pl.BlockSpec((tm, d), lambda i, j: (i, 0)),
        out_shape=jax.ShapeDtypeStruct((t, d), F32),
        compiler_params=_cparams(("arbitrary", "arbitrary")),
        name="ffn_dense",
    )(h2, x1, mod_l, wg, wu, wd)


def _dispatch_kernel(d1_ref, d2_ref, tail_ref, h_ref, xs_ref, zero_ref, sem, zsem):
    tm = h_ref.shape[0]
    base = pl.program_id(0) * tm
    blk = zero_ref.shape[0]

    @pl.when(pl.program_id(0) == 0)
    def _():
        zero_ref[...] = jnp.zeros(zero_ref.shape, zero_ref.dtype)

        def fill(e):
            return pltpu.make_async_copy(zero_ref, xs_ref.at[pl.ds(pl.multiple_of(tail_ref[e], blk), blk), :], zsem)

        for e in range(tail_ref.shape[0]):
            @pl.when(tail_ref[e] >= 0)
            def _():
                fill(e).start()

        for e in range(tail_ref.shape[0]):
            @pl.when(tail_ref[e] >= 0)
            def _():
                fill(e).wait()

    def copy(r, dst):
        return pltpu.make_async_copy(h_ref.at[pl.ds(r, 1), :], xs_ref.at[pl.ds(dst, 1), :], sem)

    def start(r, c):
        copy(r, d1_ref[base + r]).start()
        copy(r, d2_ref[base + r]).start()
        return c

    def wait(r, c):
        copy(r, d1_ref[base + r]).wait()
        copy(r, d2_ref[base + r]).wait()
        return c

    lax.fori_loop(0, tm, start, 0, unroll=ROW_DMA_UNROLL)
    lax.fori_loop(0, tm, wait, 0, unroll=ROW_DMA_UNROLL)


def _dispatch(dest1, dest2, tail, h2, n_rows, blk):
    t, d = h2.shape
    tm = min(ROW_TM, t)
    return pl.pallas_call(
        _dispatch_kernel,
        grid_spec=pltpu.PrefetchScalarGridSpec(
            num_scalar_prefetch=3,
            grid=(t // tm,),
            in_specs=[pl.BlockSpec((tm, d), lambda i, d1, d2, tl: (i, 0))],
            out_specs=pl.BlockSpec(memory_space=pl.ANY),
            scratch_shapes=[pltpu.VMEM((blk, d), h2.dtype), pltpu.SemaphoreType.DMA(()), pltpu.SemaphoreType.DMA(())],
        ),
        out_shape=jax.ShapeDtypeStruct((n_rows, d), h2.dtype),
        compiler_params=_cparams(("arbitrary",)),
        name="moe_dispatch",
    )(dest1, dest2, tail, h2)


def _expert_kernel(be_ref, nu_ref, x_ref, wg_ref, wu_ref, wd_ref, o_ref, xb_ref):
    b = pl.program_id(0)
    j = pl.program_id(1)

    @pl.when(j == 0)
    def _():
        o_ref[...] = jnp.zeros(o_ref.shape, o_ref.dtype)
        xb_ref[...] = x_ref[...].astype(BF16)

    @pl.when(b < nu_ref[0])
    def _():
        xb = xb_ref[...]
        act = (_silu(_dot(xb, wg_ref[0])) * _dot(xb, wu_ref[0])).astype(BF16)
        o_ref[...] += _dot(act, wd_ref[0])


def _experts(block_e, n_used, xs, wg, wu, wd):
    n_rows, d = xs.shape
    ne, _, ff = wg.shape
    blk = min(MOE_BLK, n_rows)
    tf = min(MOE_TF, ff)
    nb = n_rows // blk
    nj = ff // tf

    def last_live(b, nu):
        return jnp.maximum(jnp.minimum(b, nu[0] - 1), 0)

    def row_map(b, j, be, nu):
        return (last_live(b, nu), 0)

    def col_map(b, j, be, nu):
        return (be[last_live(b, nu)], 0, jnp.where(b < nu[0], j, nj - 1))

    def down_map(b, j, be, nu):
        return (be[last_live(b, nu)], jnp.where(b < nu[0], j, nj - 1), 0)

    return pl.pallas_call(
        _expert_kernel,
        grid_spec=pltpu.PrefetchScalarGridSpec(
            num_scalar_prefetch=2,
            grid=(nb, nj),
            in_specs=[
                pl.BlockSpec((blk, d), row_map),
                pl.BlockSpec((1, d, tf), col_map),
                pl.BlockSpec((1, d, tf), col_map),
                pl.BlockSpec((1, tf, d), down_map),
            ],
            out_specs=pl.BlockSpec((blk, d), lambda b, j, be, nu: (b, 0)),
            scratch_shapes=[pltpu.VMEM((blk, d), BF16)],
        ),
        out_shape=jax.ShapeDtypeStruct((n_rows, d), F32),
        compiler_params=_cparams(("arbitrary", "arbitrary")),
        name="moe_experts",
    )(block_e, n_used, xs, wg, wu, wd)


def _combine_kernel(d1_ref, d2_ref, yb_ref, route_ref, x1_ref, mod_ref, o_ref, buf_ref, sem):
    tm, d = x1_ref.shape
    i = pl.program_id(0)

    def copy(step, slot, r):
        par = step % 2
        idx = (d1_ref, d2_ref)[slot][step * tm + r]
        return pltpu.make_async_copy(yb_ref.at[pl.ds(idx, 1), :], buf_ref.at[par, slot, pl.ds(r, 1), :], sem.at[par])

    def issue(step):
        def start(r, c):
            copy(step, 0, r).start()
            copy(step, 1, r).start()
            return c
        lax.fori_loop(0, tm, start, 0, unroll=ROW_DMA_UNROLL)

    @pl.when(i == 0)
    def _():
        issue(0)

    @pl.when(i + 1 < pl.num_programs(0))
    def _():
        issue(i + 1)

    def wait(r, c):
        copy(i, 0, r).wait()
        copy(i, 1, r).wait()
        return c

    lax.fori_loop(0, tm, wait, 0, unroll=ROW_DMA_UNROLL)
    w1 = route_ref[:, 2:3]
    w2 = route_ref[:, 3:4]
    par = i % 2
    y = buf_ref[par, 0] * w1 + buf_ref[par, 1] * w2
    o_ref[...] = x1_ref[...] + mod_ref[:, 5 * d:6 * d] * y


def _combine(dest1, dest2, yb, route, x1, mod_l):
    t, d = x1.shape
    tm = min(ROW_TM, t)
    row = lambda w: pl.BlockSpec((tm, w), lambda i, d1, d2: (i, 0))
    return pl.pallas_call(
        _combine_kernel,
        grid_spec=pltpu.PrefetchScalarGridSpec(
            num_scalar_prefetch=2,
            grid=(t // tm,),
            in_specs=[pl.BlockSpec(memory_space=pl.ANY), row(LANES), row(d),
                      pl.BlockSpec(mod_l.shape, lambda i, d1, d2: (0, 0))],
            out_specs=row(d),
            scratch_shapes=[pltpu.VMEM((2, 2, tm, d), F32), pltpu.SemaphoreType.DMA((2,))],
        ),
        out_shape=jax.ShapeDtypeStruct((t, d), F32),
        compiler_params=_cparams(("arbitrary",)),
        name="moe_combine",
    )(dest1, dest2, yb, route, x1, mod_l)


def _moe_plan(route, cnt, t, blk):
    e1 = route[:, 0].astype(jnp.int32)
    e2 = route[:, 1].astype(jnp.int32)
    r1 = route[:, 4].astype(jnp.int32)
    r2 = route[:, 5].astype(jnp.int32)
    counts = cnt[0, :N_EXPERTS].astype(jnp.int32)
    padded = (counts + blk - 1) // blk * blk
    pend = jnp.cumsum(padded)
    pstart = pend - padded
    eids = jnp.arange(N_EXPERTS, dtype=jnp.int32)
    dest1 = jnp.sum(jnp.where(e1[:, None] == eids[None, :], pstart[None, :], 0), axis=1) + r1
    dest2 = jnp.sum(jnp.where(e2[:, None] == eids[None, :], pstart[None, :], 0), axis=1) + r2
    n_blocks = -(-(2 * t) // blk) + N_EXPERTS
    bstart = jnp.arange(n_blocks, dtype=jnp.int32) * blk
    block_e = jnp.minimum(jnp.sum((pend[None, :] <= bstart[:, None]).astype(jnp.int32), axis=1), N_EXPERTS - 1)
    n_used = (pend[-1] // blk).astype(jnp.int32).reshape(1)
    tail = jnp.where(padded > 0, pend - blk, -1)
    spare = bstart[(2 * t) // blk:]
    tail = jnp.concatenate([tail, jnp.where(spare >= pend[-1], spare, -1)]).astype(jnp.int32)
    return (dest1.astype(jnp.int32), dest2.astype(jnp.int32), block_e.astype(jnp.int32), n_used, tail,
            n_blocks * blk)


def _prep_mixer_weights(w_in, cq_norm, ckv_norm, w_uq, w_ukv, q_norm, k_norm):
    s1 = Q_LORA_RANK
    s2 = s1 + KV_LORA_RANK
    s3 = s2 + QK_ROPE_DIM
    half = QK_ROPE_DIM // 2
    swap = np.concatenate([np.arange(half, QK_ROPE_DIM), np.arange(half)])
    w_kr = w_in[:, s2:s3]
    uq = w_uq.reshape(Q_LORA_RANK, MLA_HEADS, QK_HEAD_DIM)
    uq_rope = uq[:, :, QK_NOPE_DIM:]
    ukv = w_ukv.reshape(KV_LORA_RANK, MLA_HEADS, QK_NOPE_DIM + V_HEAD_DIM)
    qg_r = q_norm[QK_NOPE_DIM:]
    kg_r = k_norm[QK_NOPE_DIM:]
    reps = LANES // QK_ROPE_DIM
    return {
        "w_in": jnp.concatenate([w_in[:, :s2], w_in[:, s3:]], axis=1).astype(BF16),
        "w_kr": jnp.concatenate([w_kr, w_kr[:, swap]], axis=1).T.astype(BF16),
        "cq_g": cq_norm.reshape(1, -1),
        "ckv_g": ckv_norm.reshape(1, -1),
        "w_uq": jnp.concatenate([uq[:, :, :QK_NOPE_DIM].reshape(Q_LORA_RANK, -1),
                                 uq_rope.reshape(Q_LORA_RANK, -1),
                                 uq_rope[:, :, swap].reshape(Q_LORA_RANK, -1)], axis=1).astype(BF16),
        "w_uv": ukv[:, :, QK_NOPE_DIM:].reshape(KV_LORA_RANK, -1).astype(BF16),
        "w_uk": ukv[:, :, :QK_NOPE_DIM].reshape(KV_LORA_RANK, -1).T.astype(BF16),
        "qg_n": q_norm[:QK_NOPE_DIM].reshape(1, -1),
        "qg_r": jnp.tile(qg_r, reps).reshape(1, -1),
        "qg_s": jnp.tile(qg_r[swap], reps).reshape(1, -1),
        "kg_n": k_norm[:QK_NOPE_DIM].reshape(-1, 1),
        "kg_r": kg_r.reshape(-1, 1),
        "kg_s": kg_r[swap].reshape(-1, 1),
    }


BF16_ROWS = 16


def _row_blocked(w, n_qt):
    steps = MLA_HEADS * n_qt
    tiles = w.shape[0] // BF16_ROWS
    n_blk = max(k for k in range(1, steps + 1) if tiles % k == 0)
    return w, (w.shape[0] // n_blk, w.shape[1]), lambda h, i: (jnp.minimum(h * n_qt + i, n_blk - 1), 0)


def _expert_blocked(w, n_qt):
    assert w.shape[0] == MLA_HEADS and w.shape[1] % (n_qt * BF16_ROWS) == 0
    return w, (1, w.shape[1] // n_qt, w.shape[2]), lambda h, i: (h, i, 0)


def _fnet_factors(t):
    n2 = 1 << (int(math.log2(t)) // 2)
    return t // n2, n2


def kernel(x, c, positions, ada_w, ada_b, mix_norm, w_in, cq_norm, ckv_norm, w_uq, w_ukv, q_norm, k_norm, w_fnet,
           out_norm, w_out, ffn_norm, dense_w_gate, dense_w_up, dense_w_down, router_w, moe_w_gate, moe_w_up,
           moe_w_down):
    b, s, d = x.shape
    assert b == 1
    t = b * s
    depth = ada_w.shape[0]
    assert depth == 2 and dense_w_gate.shape[0] == 1 and moe_w_gate.shape[0] == 1
    x2d = x.reshape(t, d)
    mod = _ada_mod(c.reshape(d, 1), ada_w, ada_b.reshape(depth, 1, -1))
    cos, sin = _rope_tables(positions.reshape(t).astype(F32))
    n1, n2 = _fnet_factors(t)
    n_qt = t // min(ATT_TQ, t)
    side_jobs = [
        {"w_out": _row_blocked(w_out[0], n_qt), "dense_gate": _row_blocked(dense_w_gate[0], n_qt),
         "dense_up": _row_blocked(dense_w_up[0], n_qt), "dense_down": _row_blocked(dense_w_down[0], n_qt),
         "moe_gate": _expert_blocked(moe_w_gate[0], n_qt)},
        {"w_out": _row_blocked(w_out[1], n_qt), "moe_up": _expert_blocked(moe_w_up[0], n_qt),
         "moe_down": _expert_blocked(moe_w_down[0], n_qt)},
    ]
    cast = {}
    for l in range(depth):
        mod_l = mod[l]
        wp = _prep_mixer_weights(w_in[l], cq_norm[l], ckv_norm[l], w_uq[l], w_ukv[l], q_norm[l], k_norm[l])
        fast = (_score_bound(q_norm[l], k_norm[l]) <= ATT_FAST_BOUND).astype(jnp.int32).reshape(1)
        q, kt, v, u = _mix_in(x2d, mod_l, mix_norm[l].reshape(1, d), wp, cos, sin)
        a, *new_cast = _attention(fast, q, kt, v, list(side_jobs[l].values()))
        cast.update(zip(side_jobs[l].keys(), new_cast))
        f = _fnet(u, w_fnet[l].astype(BF16), n1, n2)
        og = out_norm[l].reshape(1, d)
        fg = ffn_norm[l].reshape(1, d)
        w_out_b = cast["w_out"]
        j = l // 2
        if l % 2 == 0:
            x1, h2 = _mix_out(a, f, x2d, mod_l, og, w_out_b, fg)
            x2d = _ffn_dense(h2, x1, mod_l, cast["dense_gate"], cast["dense_up"], cast["dense_down"])
        else:
            wr = jnp.pad(router_w[j], ((0, 0), (0, LANES - N_EXPERTS)))
            wr_hi = wr.astype(BF16)
            router = jnp.concatenate([wr_hi, (wr - wr_hi.astype(F32)).astype(BF16)], axis=1)
            x1, h2, route, cnt = _mix_out(a, f, x2d, mod_l, og, w_out_b, fg, router=router)
            dest1, dest2, block_e, n_used, tail, n_rows = _moe_plan(route, cnt, t, MOE_BLK)
            xs = _dispatch(dest1, dest2, tail, h2, n_rows, MOE_BLK)
            yb = _experts(block_e, n_used, xs, cast["moe_gate"], cast["moe_up"], cast["moe_down"])
            x2d = _combine(dest1, dest2, yb, route, x1, mod_l)
    return x2d.reshape(b, s, d)
```

```python
import functools
import math

import numpy as np
import jax
import jax.numpy as jnp
from jax import lax
from jax.experimental import pallas as pl
from jax.experimental.pallas import tpu as pltpu

D_MODEL = 2048
DEPTH = 2
MLA_HEADS = 8
QK_NOPE_DIM = 128
QK_ROPE_DIM = 64
V_HEAD_DIM = 128
QK_HEAD_DIM = QK_NOPE_DIM + QK_ROPE_DIM
Q_LORA_RANK = 512
KV_LORA_RANK = 256
MLA_WIDTH = MLA_HEADS * V_HEAD_DIM
FNET_GROUPS = 8
FNET_GROUP_DIM = 128
FNET_WIDTH = FNET_GROUPS * FNET_GROUP_DIM
ROPE_THETA = 10000.0
N_EXPERTS = 8
N_MOD = 6
RMS_EPS = 1e-6

LANES = 128
QK_PAD = 256
V_PAD = 256
VMEM_LIMIT_BYTES = 56 * 1024 * 1024
ATT_VMEM_LIMIT_BYTES = 60 * 1024 * 1024

ADA_TN = 1024
ROPE_TN = 2048
MIX_TM = 512
ATT_TQ = 1024
ATT_TC = 1024
ATT_UNROLL = 16
FNET_K1 = 4
FNET_K2 = 8
FFN_TM = 512
FFN_TF = 512
MOE_BLK = 512
MOE_TF = 1024
ROW_TM = 512
ROW_DMA_UNROLL = 8

BF16 = jnp.bfloat16
F32 = jnp.float32
NT_DIMS = (((1,), (1,)), ((), ()))


def _cparams(sem, vmem=VMEM_LIMIT_BYTES):
    return pltpu.CompilerParams(dimension_semantics=sem, vmem_limit_bytes=vmem)


def _dot(a, b):
    return jnp.dot(a, b, preferred_element_type=F32)


def _rms(x, g):
    ms = jnp.mean(x * x, axis=-1, keepdims=True)
    return (x * lax.rsqrt(ms + RMS_EPS)) * g


def _ada_kernel(c_ref, w_ref, b_ref, o_ref, cond_ref):
    d = w_ref.shape[1]
    tn = w_ref.shape[2]
    rows = 64

    @pl.when(jnp.logical_and(pl.program_id(0) == 0, pl.program_id(1) == 0))
    def _():
        c = c_ref[...]
        cond_ref[...] = jnp.broadcast_to(c * (1.0 / (1.0 + jnp.exp(-c))), cond_ref.shape)

    def body(i, acc):
        r = pl.multiple_of(i * rows, rows)
        cond = cond_ref[pl.ds(r, rows), :]
        for s in range(rows // 8):
            cs = cond[8 * s:8 * s + 8]
            w8 = w_ref[0, pl.ds(r + 8 * s, 8), :]
            acc = acc + w8 * jnp.concatenate([cs] * (tn // LANES), axis=1)
        return acc

    acc = lax.fori_loop(0, d // rows, body, jnp.zeros((8, tn), F32))
    o_ref[0] = jnp.sum(acc, axis=0, keepdims=True) + b_ref[0]


def _ada_mod(c_col, ada_w, ada_b3):
    depth, d, n = ada_w.shape
    tn = min(ADA_TN, n)
    return pl.pallas_call(
        _ada_kernel,
        grid=(depth, n // tn),
        in_specs=[
            pl.BlockSpec((d, 1), lambda l, j: (0, 0)),
            pl.BlockSpec((1, d, tn), lambda l, j: (l, 0, j)),
            pl.BlockSpec((1, 1, tn), lambda l, j: (l, 0, j)),
        ],
        out_specs=pl.BlockSpec((1, 1, tn), lambda l, j: (l, 0, j)),
        out_shape=jax.ShapeDtypeStruct((depth, 1, n), F32),
        scratch_shapes=[pltpu.VMEM((d, LANES), F32)],
        compiler_params=_cparams(("arbitrary", "arbitrary")),
        name="ada_mod",
    )(c_col, ada_w, ada_b3)


def _rope_kernel(pos_ref, invf_ref, cos_ref, sin_ref):
    ang = invf_ref[...] * pos_ref[...]
    cos_ref[...] = jnp.cos(ang)
    sin_ref[...] = jnp.sin(ang)


def _rope_tables(pos_f32):
    t = pos_f32.shape[0]
    tn = min(ROPE_TN, t)
    half = QK_ROPE_DIM // 2
    inv_freq = ROPE_THETA ** (-jnp.arange(half, dtype=F32) / half)
    return pl.pallas_call(
        _rope_kernel,
        grid=(t // tn,),
        in_specs=[pl.BlockSpec((1, tn), lambda i: (0, i)), pl.BlockSpec((half, 1), lambda i: (0, 0))],
        out_specs=[pl.BlockSpec((half, tn), lambda i: (0, i))] * 2,
        out_shape=[jax.ShapeDtypeStruct((half, t), F32)] * 2,
        compiler_params=_cparams(("arbitrary",)),
        name="rope_tables",
    )(pos_f32.reshape(1, t), inv_freq.reshape(half, 1))


Q_SCALE = (QK_HEAD_DIM ** -0.5) * math.log2(math.e)


def _mix_in_kernel(x_ref, mod_ref, ng_ref, w_in_ref, w_kr_ref, cqg_ref, ckvg_ref, w_uq_ref, w_uv_ref, w_uk_ref,
                   qgn_ref, qgr_ref, qgs_ref, kgn_ref, kgr_ref, kgs_ref, cos_ref, sin_ref,
                   q_ref, kt_ref, v_ref, u_ref):
    d = x_ref.shape[1]
    cos = cos_ref[...]
    sin = sin_ref[...]
    cos_t = jnp.concatenate([cos, cos], axis=0)
    sin_t = jnp.concatenate([-sin, sin], axis=0)
    cos4 = jnp.concatenate([cos_t, cos_t], axis=0).T
    sin4 = jnp.concatenate([sin_t, sin_t], axis=0).T
    x = x_ref[...]
    shift = mod_ref[:, 0:d]
    scale = mod_ref[:, d:2 * d]
    h = _rms(x, ng_ref[...]) * (1.0 + scale) + shift
    hb = h.astype(BF16)
    z = _dot(hb, w_in_ref[...])
    c_q = z[:, 0:Q_LORA_RANK]
    c_kv = z[:, Q_LORA_RANK:Q_LORA_RANK + KV_LORA_RANK]
    u_ref[...] = z[:, Q_LORA_RANK + KV_LORA_RANK:]
    cqn = _rms(c_q, cqg_ref[...]).astype(BF16)
    ckvn = _rms(c_kv, ckvg_ref[...]).astype(BF16)

    kn_t = lax.dot_general(w_uk_ref[...], ckvn, NT_DIMS, preferred_element_type=F32)
    kr_t = lax.dot_general(w_kr_ref[...], hb, NT_DIMS, preferred_element_type=F32)
    kr = kr_t[0:QK_ROPE_DIM]
    kr_rot = (kr * kgr_ref[...]) * cos_t + (kr_t[QK_ROPE_DIM:] * kgs_ref[...]) * sin_t
    ss_r = jnp.sum(kr * kr, axis=0, keepdims=True)
    tm = x.shape[0]
    pad_rows = jnp.zeros((QK_PAD - QK_HEAD_DIM, tm), BF16)
    for hd in range(MLA_HEADS):
        kn = kn_t[hd * QK_NOPE_DIM:(hd + 1) * QK_NOPE_DIM]
        ss = jnp.sum(kn * kn, axis=0, keepdims=True) + ss_r
        r = lax.rsqrt(ss * (1.0 / QK_HEAD_DIM) + RMS_EPS)
        kt_ref[hd, 0:QK_NOPE_DIM, :] = ((kn * r) * kgn_ref[...]).astype(BF16)
        kt_ref[hd, QK_NOPE_DIM:QK_HEAD_DIM, :] = (kr_rot * r).astype(BF16)
        kt_ref[hd, QK_HEAD_DIM:, :] = pad_rows

    vz = _dot(ckvn, w_uv_ref[...])
    ones = jnp.ones((tm, V_PAD - V_HEAD_DIM), BF16)
    for hd in range(MLA_HEADS):
        v_ref[hd, :, 0:V_HEAD_DIM] = vz[:, hd * V_HEAD_DIM:(hd + 1) * V_HEAD_DIM].astype(BF16)
        v_ref[hd, :, V_HEAD_DIM:] = ones

    qz = _dot(cqn, w_uq_ref[...])
    n_nope = MLA_HEADS * QK_NOPE_DIM
    n_rope = MLA_HEADS * QK_ROPE_DIM
    lane = lax.broadcasted_iota(jnp.int32, (x.shape[0], LANES), 1)
    for pair in range(MLA_HEADS // 2):
        pr = qz[:, n_nope + pair * LANES:n_nope + (pair + 1) * LANES]
        ps = qz[:, n_nope + n_rope + pair * LANES:n_nope + n_rope + (pair + 1) * LANES]
        rot = (pr * qgr_ref[...]) * cos4 + (ps * qgs_ref[...]) * sin4
        pr2 = pr * pr
        for hd in (2 * pair, 2 * pair + 1):
            nope = qz[:, hd * QK_NOPE_DIM:(hd + 1) * QK_NOPE_DIM]
            own = (lane < QK_ROPE_DIM) if hd % 2 == 0 else (lane >= QK_ROPE_DIM)
            ss = jnp.sum(nope * nope + jnp.where(own, pr2, 0.0), axis=-1, keepdims=True)
            r = lax.rsqrt(ss * (1.0 / QK_HEAD_DIM) + RMS_EPS) * Q_SCALE
            q_ref[hd, :, 0:QK_NOPE_DIM] = ((nope * r) * qgn_ref[...]).astype(BF16)
            rope = rot * r
            if hd % 2 == 1:
                rope = pltpu.roll(rope, QK_ROPE_DIM, 1)
            q_ref[hd, :, QK_NOPE_DIM:] = jnp.where(lane < QK_ROPE_DIM, rope, 0.0).astype(BF16)


def _mix_in(x2d, mod_l, ng, wp, cos, sin):
    t, d = x2d.shape
    tm = min(MIX_TM, t)
    hh = MLA_HEADS
    const2 = lambda i: (0, 0)
    full = lambda a: pl.BlockSpec(a.shape, const2)
    ins = [x2d, mod_l, ng, wp["w_in"], wp["w_kr"], wp["cq_g"], wp["ckv_g"], wp["w_uq"], wp["w_uv"], wp["w_uk"],
           wp["qg_n"], wp["qg_r"], wp["qg_s"], wp["kg_n"], wp["kg_r"], wp["kg_s"]]
    in_specs = [pl.BlockSpec((tm, d), lambda i: (i, 0))] + [full(a) for a in ins[1:]]
    ins += [cos, sin]
    in_specs += [pl.BlockSpec((QK_ROPE_DIM // 2, tm), lambda i: (0, i))] * 2
    return pl.pallas_call(
        _mix_in_kernel,
        grid=(t // tm,),
        in_specs=in_specs,
        out_specs=[
            pl.BlockSpec((hh, tm, QK_PAD), lambda i: (0, i, 0)),
            pl.BlockSpec((hh, QK_PAD, tm), lambda i: (0, 0, i)),
            pl.BlockSpec((hh, tm, V_PAD), lambda i: (0, i, 0)),
            pl.BlockSpec((tm, FNET_WIDTH), lambda i: (i, 0)),
        ],
        out_shape=[
            jax.ShapeDtypeStruct((hh, t, QK_PAD), BF16),
            jax.ShapeDtypeStruct((hh, QK_PAD, t), BF16),
            jax.ShapeDtypeStruct((hh, t, V_PAD), BF16),
            jax.ShapeDtypeStruct((t, FNET_WIDTH), F32),
        ],
        compiler_params=_cparams(("arbitrary",)),
        name="mix_in",
    )(*ins)


ATT_FAST_BOUND = 48.0


def _score_bound(q_gain, k_gain):
    return (Q_SCALE * QK_HEAD_DIM) * jnp.max(jnp.abs(q_gain)) * jnp.max(jnp.abs(k_gain))


def _attn_kernel(fast_ref, q_ref, kt_ref, v_ref, *rest, tc, unroll, n_side):
    side_in = rest[:n_side]
    o_ref = rest[n_side]
    side_out = rest[n_side + 1:2 * n_side + 1]
    acc_ref, m_ref = rest[2 * n_side + 1:]
    n_chunks = kt_ref.shape[2] // tc
    acc_ref[...] = jnp.zeros(acc_ref.shape, F32)

    for wi_ref, wo_ref in zip(side_in, side_out):
        wo_ref[...] = wi_ref[...].astype(BF16)

    def chunk(c):
        off = pl.multiple_of(c * tc, tc)
        return kt_ref[0, :, pl.ds(off, tc)], v_ref[0, pl.ds(off, tc), :]

    @pl.when(fast_ref[0] == 1)
    def _():
        def body(c2, carry):
            for u in range(unroll):
                k_c, v_c = chunk(unroll * c2 + u)
                p = jnp.exp2(_dot(q_ref[0], k_c)).astype(BF16)
                acc_ref[...] += _dot(p, v_c)
            return carry

        lax.fori_loop(0, n_chunks // unroll, body, 0)

    @pl.when(fast_ref[0] != 1)
    def _():
        m_ref[...] = jnp.full(m_ref.shape, -jnp.inf, F32)

        def body(c, carry):
            k_c, v_c = chunk(c)
            s = _dot(q_ref[0], k_c)
            m_prev = m_ref[...]
            m_new = jnp.maximum(m_prev, jnp.max(s, axis=-1, keepdims=True))
            p = jnp.exp2(s - m_new).astype(BF16)
            acc_ref[...] = jnp.exp2(m_prev - m_new) * acc_ref[...] + _dot(p, v_c)
            m_ref[...] = m_new
            return carry

        lax.fori_loop(0, n_chunks, body, 0)

    o_ref[...] = (acc_ref[:, 0:V_HEAD_DIM] * (1.0 / acc_ref[:, V_HEAD_DIM:])).astype(o_ref.dtype)


def _attention(fast, q, kt, v, side=()):
    hh, t, _ = q.shape
    tq = min(ATT_TQ, t)
    tc = min(ATT_TC, t)
    unroll = min(ATT_UNROLL, t // tc)
    once = pl.Buffered(1)
    side_specs = [pl.BlockSpec(blk, (lambda h, i, f, im=im: im(h, i))) for _, blk, im in side]
    outs = pl.pallas_call(
        functools.partial(_attn_kernel, tc=tc, unroll=unroll, n_side=len(side)),
        grid_spec=pltpu.PrefetchScalarGridSpec(
            num_scalar_prefetch=1,
            grid=(hh, t // tq),
            in_specs=[
                pl.BlockSpec((1, tq, QK_PAD), lambda h, i, f: (h, i, 0)),
                pl.BlockSpec((1, QK_PAD, t), lambda h, i, f: (h, 0, 0), pipeline_mode=once),
                pl.BlockSpec((1, t, V_PAD), lambda h, i, f: (h, 0, 0), pipeline_mode=once),
            ] + side_specs,
            out_specs=[pl.BlockSpec((tq, V_HEAD_DIM), lambda h, i, f: (i, h))] + side_specs,
            scratch_shapes=[pltpu.VMEM((tq, V_PAD), F32), pltpu.VMEM((tq, 1), F32)],
        ),
        out_shape=[jax.ShapeDtypeStruct((t, hh * V_HEAD_DIM), BF16)]
        + [jax.ShapeDtypeStruct(w.shape, BF16) for w, _, _ in side],
        compiler_params=_cparams(("arbitrary", "arbitrary"), vmem=ATT_VMEM_LIMIT_BYTES),
        name="attention",
    )(fast, q, kt, v, *[w for w, _, _ in side])
    return outs


def _fnet1_kernel(u_ref, f1_ref, twc_ref, tws_ref, zr_ref, zi_ref, ubuf_ref, sem):
    k1, n2, width = zr_ref.shape
    i = pl.program_id(0)
    n_steps = pl.num_programs(0)

    def fetch(step, slot, ii):
        return pltpu.make_async_copy(u_ref.at[:, step * k1 + ii, :], ubuf_ref.at[slot, ii], sem.at[slot])

    @pl.when(i == 0)
    def _():
        for ii in range(k1):
            fetch(0, 0, ii).start()

    @pl.when(i + 1 < n_steps)
    def _():
        for ii in range(k1):
            fetch(i + 1, (i + 1) % 2, ii).start()

    slot = i % 2
    for ii in range(k1):
        fetch(i, slot, ii).wait()
    for ii in range(k1):
        y = _dot(f1_ref[...], ubuf_ref[slot, ii].astype(BF16))
        a = y[0:n2]
        b = y[n2:]
        c = twc_ref[ii]
        s = tws_ref[ii]
        zr_ref[ii] = (a * c - b * s).astype(zr_ref.dtype)
        zi_ref[ii] = (-(b * c) - a * s).astype(zi_ref.dtype)


def _fnet2_kernel(zr_ref, zi_ref, f2_ref, cs_ref, wf_ref, o_ref):
    n1, k2, _ = o_ref.shape
    gd = FNET_GROUP_DIM
    zz = jnp.concatenate([zr_ref[...], zi_ref[...]], axis=0)
    x = _dot(f2_ref[...], zz)
    xr = x[0:n1]
    xi = x[n1:]
    for g in range(FNET_GROUPS):
        rows = []
        for j in range(k2):
            c0 = j * FNET_WIDTH + g * gd
            rows.append(jnp.concatenate([xr[:, c0:c0 + gd], xi[:, c0:c0 + gd]], axis=1))
        lhs = jnp.concatenate(rows, axis=0).astype(BF16)
        fg = _dot(lhs, cs_ref[...]).astype(BF16)
        og = _dot(fg, wf_ref[g])
        for j in range(k2):
            o_ref[:, j, g * gd:(g + 1) * gd] = og[j * n1:(j + 1) * n1].astype(o_ref.dtype)


def _dft_consts(n1, n2):
    n = n1 * n2
    a2 = 2.0 * np.pi * np.outer(np.arange(n2), np.arange(n2)) / n2
    f1 = np.concatenate([np.cos(a2), np.sin(a2)], axis=0) / n2
    atw = 2.0 * np.pi * np.outer(np.arange(n1), np.arange(n2)) / n
    a1 = 2.0 * np.pi * np.outer(np.arange(n1), np.arange(n1)) / n1
    c1, s1 = np.cos(a1), np.sin(a1)
    f2 = np.block([[c1, s1], [-s1, c1]])
    ac = 2.0 * np.pi * np.outer(np.arange(FNET_GROUP_DIM), np.arange(FNET_GROUP_DIM)) / FNET_GROUP_DIM
    norm = n2 / math.sqrt(n * FNET_GROUP_DIM)
    cs = np.concatenate([np.cos(ac), np.sin(ac)], axis=0) * norm
    return (jnp.asarray(f1, BF16), jnp.asarray(np.cos(atw)[:, :, None], F32), jnp.asarray(np.sin(atw)[:, :, None], F32),
            jnp.asarray(f2, BF16), jnp.asarray(cs, BF16))


def _fnet(u, wf_b, n1, n2):
    t, width = u.shape
    assert t == n1 * n2
    f1, twc, tws, f2, cs = _dft_consts(n1, n2)
    k1 = min(FNET_K1, n1)
    k2 = min(FNET_K2, n2)
    const2 = lambda i: (0, 0)
    zr, zi = pl.pallas_call(
        _fnet1_kernel,
        grid=(n1 // k1,),
        in_specs=[
            pl.BlockSpec(memory_space=pl.ANY),
            pl.BlockSpec(f1.shape, const2),
            pl.BlockSpec((k1, n2, 1), lambda i: (i, 0, 0)),
            pl.BlockSpec((k1, n2, 1), lambda i: (i, 0, 0)),
        ],
        out_specs=[pl.BlockSpec((k1, n2, width), lambda i: (i, 0, 0))] * 2,
        out_shape=[jax.ShapeDtypeStruct((n1, n2, width), BF16)] * 2,
        scratch_shapes=[pltpu.VMEM((2, k1, n2, width), u.dtype), pltpu.SemaphoreType.DMA((2,))],
        compiler_params=_cparams(("arbitrary",)),
        name="fnet_stage1",
    )(u.reshape(n2, n1, width), f1, twc, tws)
    out = pl.pallas_call(
        _fnet2_kernel,
        grid=(n2 // k2,),
        in_specs=[
            pl.BlockSpec((n1, k2 * width), lambda i: (0, i)),
            pl.BlockSpec((n1, k2 * width), lambda i: (0, i)),
            pl.BlockSpec(f2.shape, const2),
            pl.BlockSpec(cs.shape, const2),
            pl.BlockSpec(wf_b.shape, lambda i: (0, 0, 0)),
        ],
        out_specs=pl.BlockSpec((n1, k2, width), lambda i: (0, i, 0)),
        out_shape=jax.ShapeDtypeStruct((n1, n2, width), F32),
        compiler_params=_cparams(("arbitrary",)),
        name="fnet_stage2",
    )(zr.reshape(n1, n2 * width), zi.reshape(n1, n2 * width), f2, cs, wf_b)
    return out.reshape(t, width)


def _mix_out_kernel(a_ref, f_ref, x_ref, mod_ref, og_ref, w_out_ref, fg_ref, *rest, moe):
    d = x_ref.shape[1]
    half = a_ref.shape[1]
    an = _rms(a_ref[...].astype(F32), og_ref[:, 0:half])
    fn = _rms(f_ref[...].astype(F32), og_ref[:, half:])
    cat = jnp.concatenate([an, fn], axis=1).astype(BF16)
    o = _dot(cat, w_out_ref[...])
    gate1 = mod_ref[:, 2 * d:3 * d]
    shift2 = mod_ref[:, 3 * d:4 * d]
    scale2 = mod_ref[:, 4 * d:5 * d]
    x1 = x_ref[...] + gate1 * o
    h2 = _rms(x1, fg_ref[...]) * (1.0 + scale2) + shift2
    if not moe:
        x1_ref, h2_ref = rest
        x1_ref[...] = x1
        h2_ref[...] = h2.astype(h2_ref.dtype)
        return
    wr_ref, tri_ref, x1_ref, h2_ref, route_ref, cnt_ref, carry_ref = rest
    x1_ref[...] = x1
    h2_ref[...] = h2

    @pl.when(pl.program_id(0) == 0)
    def _():
        carry_ref[...] = jnp.zeros(carry_ref.shape, F32)

    tm = x1.shape[0]
    h_hi = h2.astype(BF16)
    h_lo = (h2 - h_hi.astype(F32)).astype(BF16)
    hw = _dot(h_hi, wr_ref[...])
    logits = (hw[:, 0:LANES] + hw[:, LANES:]) + _dot(h_lo, wr_ref[:, 0:LANES])
    lane = lax.broadcasted_iota(jnp.int32, (tm, LANES), 1)
    neg = jnp.float32(-jnp.inf)
    logits = jnp.where(lane < N_EXPERTS, logits, neg)
    m1 = jnp.max(logits, axis=-1, keepdims=True)
    i1 = jnp.min(jnp.where(logits == m1, lane, LANES), axis=-1, keepdims=True)
    rest_l = jnp.where(lane == i1, neg, logits)
    m2 = jnp.max(rest_l, axis=-1, keepdims=True)
    i2 = jnp.min(jnp.where(rest_l == m2, lane, LANES), axis=-1, keepdims=True)
    e21 = jnp.exp(m2 - m1)
    w1 = 1.0 / (1.0 + e21)
    w2 = e21 / (1.0 + e21)
    oh1 = jnp.where(lane == i1, 1.0, 0.0)
    oh2 = jnp.where(lane == i2, 1.0, 0.0)
    pre = _dot(tri_ref[...], jnp.concatenate([oh1, oh2], axis=1).astype(BF16))
    cnt1 = jnp.sum(oh1, axis=0, keepdims=True)
    cnt2 = jnp.sum(oh2, axis=0, keepdims=True)
    carry = carry_ref[0:1, :]
    rank1 = jnp.sum(oh1 * (carry + pre[:, 0:LANES]), axis=-1, keepdims=True)
    rank2 = jnp.sum(oh2 * (carry + cnt1 + pre[:, LANES:]), axis=-1, keepdims=True)
    new_carry = carry + cnt1 + cnt2
    carry_ref[...] = jnp.broadcast_to(new_carry, carry_ref.shape)
    cnt_ref[...] = jnp.broadcast_to(new_carry, cnt_ref.shape)
    vals = (i1.astype(F32), i2.astype(F32), w1, w2, rank1, rank2)
    route = jnp.zeros((tm, LANES), F32)
    for k, val in enumerate(vals):
        route = jnp.where(lane == k, val, route)
    route_ref[...] = route


def _mix_out(a, f, x2d, mod_l, og, w_out_b, fg, router=None):
    t, d = x2d.shape
    tm = min(MIX_TM, t)
    half = a.shape[1]
    moe = router is not None
    const2 = lambda i: (0, 0)
    row = lambda w: pl.BlockSpec((tm, w), lambda i: (i, 0))
    ins = [a, f, x2d, mod_l, og, w_out_b, fg]
    in_specs = [row(half), row(half), row(d), pl.BlockSpec(mod_l.shape, const2), pl.BlockSpec(og.shape, const2),
                pl.BlockSpec(w_out_b.shape, const2), pl.BlockSpec(fg.shape, const2)]
    out_specs = [row(d), row(d)]
    out_shape = [jax.ShapeDtypeStruct((t, d), F32), jax.ShapeDtypeStruct((t, d), F32 if moe else BF16)]
    scratch = []
    if moe:
        tri = jnp.asarray(np.tril(np.ones((tm, tm), np.float32), -1), BF16)
        ins += [router, tri]
        in_specs += [pl.BlockSpec(router.shape, const2), pl.BlockSpec(tri.shape, const2)]
        out_specs += [row(LANES), pl.BlockSpec((8, LANES), const2)]
        out_shape += [jax.ShapeDtypeStruct((t, LANES), F32), jax.ShapeDtypeStruct((8, LANES), F32)]
        scratch = [pltpu.VMEM((8, LANES), F32)]
    return pl.pallas_call(
        functools.partial(_mix_out_kernel, moe=moe),
        grid=(t // tm,),
        in_specs=in_specs,
        out_specs=out_specs,
        out_shape=out_shape,
        scratch_shapes=scratch,
        compiler_params=_cparams(("arbitrary",)),
        name="mix_out_moe" if moe else "mix_out",
    )(*ins)


def _silu(g):
    return g * (1.0 / (1.0 + jnp.exp(-g)))


def _ffn_kernel(h_ref, x1_ref, mod_ref, wg_ref, wu_ref, wd_ref, o_ref):
    j = pl.program_id(1)
    d = x1_ref.shape[1]

    @pl.when(j == 0)
    def _():
        o_ref[...] = jnp.zeros(o_ref.shape, F32)

    h = h_ref[...]
    act = (_silu(_dot(h, wg_ref[...])) * _dot(h, wu_ref[...])).astype(BF16)
    o_ref[...] += _dot(act, wd_ref[...])

    @pl.when(j == pl.num_programs(1) - 1)
    def _():
        o_ref[...] = x1_ref[...] + mod_ref[:, 5 * d:6 * d] * o_ref[...]


def _ffn_dense(h2, x1, mod_l, wg, wu, wd):
    t, d = x1.shape
    ff = wg.shape[1]
    tm = min(FFN_TM, t)
    tf = min(FFN_TF, ff)
    return pl.pallas_call(
        _ffn_kernel,
        grid=(t // tm, ff // tf),
        in_specs=[
            pl.BlockSpec((tm, d), lambda i, j: (i, 0)),
            pl.BlockSpec((tm, d), lambda i, j: (i, 0)),
            pl.BlockSpec(mod_l.shape, lambda i, j: (0, 0)),
            pl.BlockSpec((d, tf), lambda i, j: (0, j)),
            pl.BlockSpec((d, tf), lambda i, j: (0, j)),
            pl.BlockSpec((tf, d), lambda i, j: (j, 0)),
        ],
        out_specs=pl.BlockSpec((tm, d), lambda i, j: (i, 0)),
        out_shape=jax.ShapeDtypeStruct((t, d), F32),
        compiler_params=_cparams(("arbitrary", "arbitrary")),
        name="ffn_dense",
    )(h2, x1, mod_l, wg, wu, wd)


def _dispatch_kernel(d1_ref, d2_ref, tail_ref, h_ref, xs_ref, zero_ref, sem, zsem):
    tm = h_ref.shape[0]
    base = pl.program_id(0) * tm
    blk = zero_ref.shape[0]

    @pl.when(pl.program_id(0) == 0)
    def _():
        zero_ref[...] = jnp.zeros(zero_ref.shape, zero_ref.dtype)

        def fill(e):
            return pltpu.make_async_copy(zero_ref, xs_ref.at[pl.ds(pl.multiple_of(tail_ref[e], blk), blk), :], zsem)

        for e in range(tail_ref.shape[0]):
            @pl.when(tail_ref[e] >= 0)
            def _():
                fill(e).start()

        for e in range(tail_ref.shape[0]):
            @pl.when(tail_ref[e] >= 0)
            def _():
                fill(e).wait()

    def copy(r, dst):
        return pltpu.make_async_copy(h_ref.at[pl.ds(r, 1), :], xs_ref.at[pl.ds(dst, 1), :], sem)

    def start(r, c):
        copy(r, d1_ref[base + r]).start()
        copy(r, d2_ref[base + r]).start()
        return c

    def wait(r, c):
        copy(r, d1_ref[base + r]).wait()
        copy(r, d2_ref[base + r]).wait()
        return c

    lax.fori_loop(0, tm, start, 0, unroll=ROW_DMA_UNROLL)
    lax.fori_loop(0, tm, wait, 0, unroll=ROW_DMA_UNROLL)


def _dispatch(dest1, dest2, tail, h2, n_rows, blk):
    t, d = h2.shape
    tm = min(ROW_TM, t)
    return pl.pallas_call(
        _dispatch_kernel,
        grid_spec=pltpu.PrefetchScalarGridSpec(
            num_scalar_prefetch=3,
            grid=(t // tm,),
            in_specs=[pl.BlockSpec((tm, d), lambda i, d1, d2, tl: (i, 0))],
            out_specs=pl.BlockSpec(memory_space=pl.ANY),
            scratch_shapes=[pltpu.VMEM((blk, d), h2.dtype), pltpu.SemaphoreType.DMA(()), pltpu.SemaphoreType.DMA(())],
        ),
        out_shape=jax.ShapeDtypeStruct((n_rows, d), h2.dtype),
        compiler_params=_cparams(("arbitrary",)),
        name="moe_dispatch",
    )(dest1, dest2, tail, h2)


def _expert_kernel(be_ref, nu_ref, x_ref, wg_ref, wu_ref, wd_ref, o_ref, xb_ref):
    b = pl.program_id(0)
    j = pl.program_id(1)

    @pl.when(j == 0)
    def _():
        o_ref[...] = jnp.zeros(o_ref.shape, o_ref.dtype)
        xb_ref[...] = x_ref[...].astype(BF16)

    @pl.when(b < nu_ref[0])
    def _():
        xb = xb_ref[...]
        act = (_silu(_dot(xb, wg_ref[0])) * _dot(xb, wu_ref[0])).astype(BF16)
        o_ref[...] += _dot(act, wd_ref[0])


def _experts(block_e, n_used, xs, wg, wu, wd):
    n_rows, d = xs.shape
    ne, _, ff = wg.shape
    blk = min(MOE_BLK, n_rows)
    tf = min(MOE_TF, ff)
    nb = n_rows // blk
    nj = ff // tf

    def last_live(b, nu):
        return jnp.maximum(jnp.minimum(b, nu[0] - 1), 0)

    def row_map(b, j, be, nu):
        return (last_live(b, nu), 0)

    def col_map(b, j, be, nu):
        return (be[last_live(b, nu)], 0, jnp.where(b < nu[0], j, nj - 1))

    def down_map(b, j, be, nu):
        return (be[last_live(b, nu)], jnp.where(b < nu[0], j, nj - 1), 0)

    return pl.pallas_call(
        _expert_kernel,
        grid_spec=pltpu.PrefetchScalarGridSpec(
            num_scalar_prefetch=2,
            grid=(nb, nj),
            in_specs=[
                pl.BlockSpec((blk, d), row_map),
                pl.BlockSpec((1, d, tf), col_map),
                pl.BlockSpec((1, d, tf), col_map),
                pl.BlockSpec((1, tf, d), down_map),
            ],
            out_specs=pl.BlockSpec((blk, d), lambda b, j, be, nu: (b, 0)),
            scratch_shapes=[pltpu.VMEM((blk, d), BF16)],
        ),
        out_shape=jax.ShapeDtypeStruct((n_rows, d), F32),
        compiler_params=_cparams(("arbitrary", "arbitrary")),
        name="moe_experts",
    )(block_e, n_used, xs, wg, wu, wd)


def _combine_kernel(d1_ref, d2_ref, yb_ref, route_ref, x1_ref, mod_ref, o_ref, buf_ref, sem):
    tm, d = x1_ref.shape
    i = pl.program_id(0)

    def copy(step, slot, r):
        par = step % 2
        idx = (d1_ref, d2_ref)[slot][step * tm + r]
        return pltpu.make_async_copy(yb_ref.at[pl.ds(idx, 1), :], buf_ref.at[par, slot, pl.ds(r, 1), :], sem.at[par])

    def issue(step):
        def start(r, c):
            copy(step, 0, r).start()
            copy(step, 1, r).start()
            return c
        lax.fori_loop(0, tm, start, 0, unroll=ROW_DMA_UNROLL)

    @pl.when(i == 0)
    def _():
        issue(0)

    @pl.when(i + 1 < pl.num_programs(0))
    def _():
        issue(i + 1)

    def wait(r, c):
        copy(i, 0, r).wait()
        copy(i, 1, r).wait()
        return c

    lax.fori_loop(0, tm, wait, 0, unroll=ROW_DMA_UNROLL)
    w1 = route_ref[:, 2:3]
    w2 = route_ref[:, 3:4]
    par = i % 2
    y = buf_ref[par, 0] * w1 + buf_ref[par, 1] * w2
    o_ref[...] = x1_ref[...] + mod_ref[:, 5 * d:6 * d] * y


def _combine(dest1, dest2, yb, route, x1, mod_l):
    t, d = x1.shape
    tm = min(ROW_TM, t)
    row = lambda w: pl.BlockSpec((tm, w), lambda i, d1, d2: (i, 0))
    return pl.pallas_call(
        _combine_kernel,
        grid_spec=pltpu.PrefetchScalarGridSpec(
            num_scalar_prefetch=2,
            grid=(t // tm,),
            in_specs=[pl.BlockSpec(memory_space=pl.ANY), row(LANES), row(d),
                      pl.BlockSpec(mod_l.shape, lambda i, d1, d2: (0, 0))],
            out_specs=row(d),
            scratch_shapes=[pltpu.VMEM((2, 2, tm, d), F32), pltpu.SemaphoreType.DMA((2,))],
        ),
        out_shape=jax.ShapeDtypeStruct((t, d), F32),
        compiler_params=_cparams(("arbitrary",)),
        name="moe_combine",
    )(dest1, dest2, yb, route, x1, mod_l)


def _moe_plan(route, cnt, t, blk):
    e1 = route[:, 0].astype(jnp.int32)
    e2 = route[:, 1].astype(jnp.int32)
    r1 = route[:, 4].astype(jnp.int32)
    r2 = route[:, 5].astype(jnp.int32)
    counts = cnt[0, :N_EXPERTS].astype(jnp.int32)
    padded = (counts + blk - 1) // blk * blk
    pend = jnp.cumsum(padded)
    pstart = pend - padded
    eids = jnp.arange(N_EXPERTS, dtype=jnp.int32)
    dest1 = jnp.sum(jnp.where(e1[:, None] == eids[None, :], pstart[None, :], 0), axis=1) + r1
    dest2 = jnp.sum(jnp.where(e2[:, None] == eids[None, :], pstart[None, :], 0), axis=1) + r2
    n_blocks = -(-(2 * t) // blk) + N_EXPERTS
    bstart = jnp.arange(n_blocks, dtype=jnp.int32) * blk
    block_e = jnp.minimum(jnp.sum((pend[None, :] <= bstart[:, None]).astype(jnp.int32), axis=1), N_EXPERTS - 1)
    n_used = (pend[-1] // blk).astype(jnp.int32).reshape(1)
    tail = jnp.where(padded > 0, pend - blk, -1)
    spare = bstart[(2 * t) // blk:]
    tail = jnp.concatenate([tail, jnp.where(spare >= pend[-1], spare, -1)]).astype(jnp.int32)
    return (dest1.astype(jnp.int32), dest2.astype(jnp.int32), block_e.astype(jnp.int32), n_used, tail,
            n_blocks * blk)


def _prep_mixer_weights(w_in, cq_norm, ckv_norm, w_uq, w_ukv, q_norm, k_norm):
    s1 = Q_LORA_RANK
    s2 = s1 + KV_LORA_RANK
    s3 = s2 + QK_ROPE_DIM
    half = QK_ROPE_DIM // 2
    swap = np.concatenate([np.arange(half, QK_ROPE_DIM), np.arange(half)])
    w_kr = w_in[:, s2:s3]
    uq = w_uq.reshape(Q_LORA_RANK, MLA_HEADS, QK_HEAD_DIM)
    uq_rope = uq[:, :, QK_NOPE_DIM:]
    ukv = w_ukv.reshape(KV_LORA_RANK, MLA_HEADS, QK_NOPE_DIM + V_HEAD_DIM)
    qg_r = q_norm[QK_NOPE_DIM:]
    kg_r = k_norm[QK_NOPE_DIM:]
    reps = LANES // QK_ROPE_DIM
    return {
        "w_in": jnp.concatenate([w_in[:, :s2], w_in[:, s3:]], axis=1).astype(BF16),
        "w_kr": jnp.concatenate([w_kr, w_kr[:, swap]], axis=1).T.astype(BF16),
        "cq_g": cq_norm.reshape(1, -1),
        "ckv_g": ckv_norm.reshape(1, -1),
        "w_uq": jnp.concatenate([uq[:, :, :QK_NOPE_DIM].reshape(Q_LORA_RANK, -1),
                                 uq_rope.reshape(Q_LORA_RANK, -1),
                                 uq_rope[:, :, swap].reshape(Q_LORA_RANK, -1)], axis=1).astype(BF16),
        "w_uv": ukv[:, :, QK_NOPE_DIM:].reshape(KV_LORA_RANK, -1).astype(BF16),
        "w_uk": ukv[:, :, :QK_NOPE_DIM].reshape(KV_LORA_RANK, -1).T.astype(BF16),
        "qg_n": q_norm[:QK_NOPE_DIM].reshape(1, -1),
        "qg_r": jnp.tile(qg_r, reps).reshape(1, -1),
        "qg_s": jnp.tile(qg_r[swap], reps).reshape(1, -1),
        "kg_n": k_norm[:QK_NOPE_DIM].reshape(-1, 1),
        "kg_r": kg_r.reshape(-1, 1),
        "kg_s": kg_r[swap].reshape(-1, 1),
    }


BF16_ROWS = 16


def _row_blocked(w, n_qt):
    steps = MLA_HEADS * n_qt
    tiles = w.shape[0] // BF16_ROWS
    n_blk = max(k for k in range(1, steps + 1) if tiles % k == 0)
    return w, (w.shape[0] // n_blk, w.shape[1]), lambda h, i: (jnp.minimum(h * n_qt + i, n_blk - 1), 0)


def _expert_blocked(w, n_qt):
    assert w.shape[0] == MLA_HEADS and w.shape[1] % (n_qt * BF16_ROWS) == 0
    return w, (1, w.shape[1] // n_qt, w.shape[2]), lambda h, i: (h, i, 0)


def _fnet_factors(t):
    n2 = 1 << (int(math.log2(t)) // 2)
    return t // n2, n2


def kernel(x, c, positions, ada_w, ada_b, mix_norm, w_in, cq_norm, ckv_norm, w_uq, w_ukv, q_norm, k_norm, w_fnet,
           out_norm, w_out, ffn_norm, dense_w_gate, dense_w_up, dense_w_down, router_w, moe_w_gate, moe_w_up,
           moe_w_down):
    b, s, d = x.shape
    assert b == 1
    t = b * s
    depth = ada_w.shape[0]
    assert depth == 2 and dense_w_gate.shape[0] == 1 and moe_w_gate.shape[0] == 1
    x2d = x.reshape(t, d)
    mod = _ada_mod(c.reshape(d, 1), ada_w, ada_b.reshape(depth, 1, -1))
    cos, sin = _rope_tables(positions.reshape(t).astype(F32))
    n1, n2 = _fnet_factors(t)
    n_qt = t // min(ATT_TQ, t)
    side_jobs = [
        {"w_out": _row_blocked(w_out[0], n_qt), "dense_gate": _row_blocked(dense_w_gate[0], n_qt),
         "dense_up": _row_blocked(dense_w_up[0], n_qt), "dense_down": _row_blocked(dense_w_down[0], n_qt),
         "moe_gate": _expert_blocked(moe_w_gate[0], n_qt)},
        {"w_out": _row_blocked(w_out[1], n_qt), "moe_up": _expert_blocked(moe_w_up[0], n_qt),
         "moe_down": _expert_blocked(moe_w_down[0], n_qt)},
    ]
    cast = {}
    for l in range(depth):
        mod_l = mod[l]
        wp = _prep_mixer_weights(w_in[l], cq_norm[l], ckv_norm[l], w_uq[l], w_ukv[l], q_norm[l], k_norm[l])
        fast = (_score_bound(q_norm[l], k_norm[l]) <= ATT_FAST_BOUND).astype(jnp.int32).reshape(1)
        q, kt, v, u = _mix_in(x2d, mod_l, mix_norm[l].reshape(1, d), wp, cos, sin)
        a, *new_cast = _attention(fast, q, kt, v, list(side_jobs[l].values()))
        cast.update(zip(side_jobs[l].keys(), new_cast))
        f = _fnet(u, w_fnet[l].astype(BF16), n1, n2)
        og = out_norm[l].reshape(1, d)
        fg = ffn_norm[l].reshape(1, d)
        w_out_b = cast["w_out"]
        j = l // 2
        if l % 2 == 0:
            x1, h2 = _mix_out(a, f, x2d, mod_l, og, w_out_b, fg)
            x2d = _ffn_dense(h2, x1, mod_l, cast["dense_gate"], cast["dense_up"], cast["dense_down"])
        else:
            wr = jnp.pad(router_w[j], ((0, 0), (0, LANES - N_EXPERTS)))
            wr_hi = wr.astype(BF16)
            router = jnp.concatenate([wr_hi, (wr - wr_hi.astype(F32)).astype(BF16)], axis=1)
            x1, h2, route, cnt = _mix_out(a, f, x2d, mod_l, og, w_out_b, fg, router=router)
            dest1, dest2, block_e, n_used, tail, n_rows = _moe_plan(route, cnt, t, MOE_BLK)
            xs = _dispatch(dest1, dest2, tail, h2, n_rows, MOE_BLK)
            yb = _experts(block_e, n_used, xs, cast["moe_gate"], cast["moe_up"], cast["moe_down"])
            x2d = _combine(dest1, dest2, yb, route, x1, mod_l)
    return x2d.reshape(b, s, d)
```
